```python
import math
import jax
import jax.numpy as jnp
from jax import lax
import numpy as np

D_MODEL = 4096
BATCH = 2
SEQ = 8192
DEPTH = 4

CHUNK = 64
F32 = jnp.float32
TINY = 1e-30

SSD_WIDTH = D_MODEL // 2
SSD_HEAD_DIM = 64
SSD_HEADS = SSD_WIDTH // SSD_HEAD_DIM
SSD_GROUPS = 4
SSD_STATE = 128
SSD_CONV = 4
SSD_CONV_DIM = SSD_WIDTH + 2 * SSD_GROUPS * SSD_STATE

HG_WIDTH = D_MODEL // 4
HG_VDIM = 128
HG_HEADS = HG_WIDTH // HG_VDIM
HG_KDIM = 128

RET_WIDTH = D_MODEL // 4
RET_VDIM = 128
RET_HEADS = RET_WIDTH // RET_VDIM
RET_KDIM = 64
ROPE_BASE = 10000.0

IN_SIZES = (SSD_WIDTH, SSD_CONV_DIM, SSD_HEADS,
            HG_HEADS * HG_KDIM, HG_HEADS * HG_KDIM, HG_WIDTH, HG_WIDTH,
            RET_HEADS * RET_KDIM, RET_HEADS * RET_KDIM, RET_WIDTH, RET_WIDTH)
IN_WIDTH = sum(IN_SIZES)
SPLIT_POINTS = tuple(int(s) for s in np.cumsum(IN_SIZES)[:-1])

N_EXPERTS = 16
N_EXPERT_GROUPS = 4
EXPERTS_PER_GROUP = N_EXPERTS // N_EXPERT_GROUPS
TOP_K = 2
D_FF_EXPERT = 512
MOE_BLOCK = 256

ADA_RANK = 256
N_MOD = 6
DEEPNORM_ALPHA = (2.0 * DEPTH) ** 0.25
DEEPNORM_BETA = (8.0 * DEPTH) ** -0.25
NORM_EPS = 1e-5

kernel_name = 'hybrid_ssd_hgrn2_retention_moe_block'


def causal_mask():
    return jnp.tril(jnp.ones((CHUNK, CHUNK), dtype=bool))


def masked_decay(log_diff, mask):
    return jnp.where(mask, jnp.exp(jnp.minimum(log_diff, 0.0)), 0.0)


def layer_norm(x, g, b):
    xf = x.astype(F32)
    mu = jnp.mean(xf, -1, keepdims=True)
    var = jnp.mean(jnp.square(xf - mu), -1, keepdims=True)
    return ((xf - mu) * lax.rsqrt(var + NORM_EPS) * g.astype(F32) + b.astype(F32)).astype(x.dtype)


def rms_norm_groups(y, groups, w):
    shp = y.shape
    yg = y.reshape(shp[:-1] + (groups, shp[-1] // groups))
    yg = yg * lax.rsqrt(jnp.mean(jnp.square(yg), -1, keepdims=True) + NORM_EPS)
    return yg.reshape(shp) * w.astype(F32)


def causal_depthwise_conv(x, w, b):
    k = w.shape[0]
    xp = jnp.pad(x, ((0, 0), (k - 1, 0), (0, 0)))
    y = lax.conv_general_dilated(xp, w[:, None, :].astype(x.dtype), window_strides=(1,), padding='VALID',
                                 dimension_numbers=('NWC', 'WIO', 'NWC'), feature_group_count=x.shape[-1])
    return y + b.astype(x.dtype)


def exclusive_chunk_scan(chunk_states, chunk_decay):
    def step(carry, inp):
        s, d = inp
        return carry * d + s, carry
    xs = (jnp.moveaxis(chunk_states, 1, 0), jnp.moveaxis(chunk_decay, 1, 0))
    _, prev = lax.scan(step, jnp.zeros_like(chunk_states[:, 0]), xs)
    return jnp.moveaxis(prev, 0, 1)


def rotary(t, positions):
    half = t.shape[-1] // 2
    freq = ROPE_BASE ** (-jnp.arange(half, dtype=F32) / half)
    ang = positions.astype(F32)[..., None] * freq
    cos, sin = jnp.cos(ang)[:, :, None, :], jnp.sin(ang)[:, :, None, :]
    t1, t2 = t[..., :half], t[..., half:]
    return jnp.concatenate([t1 * cos - t2 * sin, t1 * sin + t2 * cos], -1)


def ssd_mixer(z, xbc, dt_raw, conv_w, conv_b, dt_bias, a_log, d_skip, norm_w):
    bsz, seq, _ = z.shape
    nc = seq // CHUNK
    g, r, p, n = SSD_GROUPS, SSD_HEADS // SSD_GROUPS, SSD_HEAD_DIM, SSD_STATE
    xbc = jax.nn.silu(causal_depthwise_conv(xbc, conv_w, conv_b)).astype(F32)
    xs = xbc[..., :SSD_WIDTH].reshape(bsz, nc, CHUNK, g, r, p)
    bm = xbc[..., SSD_WIDTH:SSD_WIDTH + g * n].reshape(bsz, nc, CHUNK, g, n)
    cm = xbc[..., SSD_WIDTH + g * n:].reshape(bsz, nc, CHUNK, g, n)
    dt = jax.nn.softplus(dt_raw.astype(F32) + dt_bias.astype(F32)).reshape(bsz, nc, CHUNK, g, r)
    a = -jnp.exp(a_log.astype(F32)).reshape(g, r)
    xdt = xs * dt[..., None]
    cum = jnp.cumsum((dt * a).transpose(0, 1, 3, 4, 2), axis=-1)
    decay = masked_decay(cum[..., :, None] - cum[..., None, :], causal_mask())
    att = jnp.einsum('bclgn,bcsgn->bcgls', cm, bm)[:, :, :, None] * decay
    y = jnp.einsum('bcgrls,bcsgrp->bclgrp', att, xdt)
    to_end = jnp.exp(cum[..., -1:] - cum).transpose(0, 1, 4, 2, 3)
    chunk_states = jnp.einsum('bcsgn,bcsgrp->bcgrpn', bm, xdt * to_end[..., None])
    chunk_decay = jnp.exp(cum[..., -1])[..., None, None]
    prev = exclusive_chunk_scan(chunk_states, chunk_decay)
    from_start = jnp.exp(cum).transpose(0, 1, 4, 2, 3)[..., None]
    y = (y + jnp.einsum('bclgn,bcgrpn->bclgrp', cm, prev) * from_start
         + xs * d_skip.astype(F32).reshape(g, r, 1))
    y = y.reshape(bsz, seq, SSD_WIDTH) * jax.nn.silu(z.astype(F32))
    return rms_norm_groups(y, SSD_GROUPS, norm_w).astype(z.dtype)


def hgrn2_mixer(q, f_raw, v, g, lower_bound, norm_w):
    bsz, seq, _ = q.shape
    nc = seq // CHUNK
    h, dk, dv = HG_HEADS, HG_KDIM, HG_VDIM

    def to_chunks(t, d):
        return t.astype(F32).reshape(bsz, nc, CHUNK, h, d).transpose(1, 0, 3, 2, 4)

    lb = lower_bound.reshape(h, dk)
    fr = f_raw.astype(F32).reshape(bsz, seq, h, dk)
    sig = jax.nn.sigmoid(fr)
    log_f = jnp.log(jnp.maximum(lb + (1.0 - lb) * sig, TINY))
    key = (1.0 - lb) * jax.nn.sigmoid(-fr)
    mask = causal_mask()[:, :, None]

    def chunk_step(state, inp):
        qc, kc, vc, lfc = inp
        bcum = jnp.cumsum(lfc, axis=2)
        o = jnp.einsum('bhld,bhdv->bhlv', qc * jnp.exp(bcum), state)
        rel = masked_decay(bcum[:, :, :, None, :] - bcum[:, :, None, :, :], mask)
        att = jnp.einsum('bhtd,bhtsd->bhts', qc, kc[:, :, None] * rel)
        o = o + jnp.einsum('bhts,bhsv->bhtv', att, vc)
        last = bcum[:, :, -1:, :]
        state = (state * jnp.exp(last[:, :, 0, :, None])
                 + jnp.einsum('bhld,bhlv->bhdv', kc * jnp.exp(last - bcum), vc))
        return state, o

    init = jnp.zeros((bsz, h, dk, dv), F32)
    _, o = lax.scan(chunk_step, init, (to_chunks(q, dk), to_chunks(key, dk), to_chunks(v, dv), to_chunks(log_f, dk)))
    o = o.transpose(1, 0, 3, 2, 4).reshape(bsz, seq, h * dv)
    o = rms_norm_groups(o, h, norm_w) * jax.nn.silu(g.astype(F32))
    return o.astype(g.dtype)


def retention_mixer(q, k, v, g, positions, norm_w, norm_b):
    bsz, seq, _ = q.shape
    nc = seq // CHUNK
    h, dk, dv = RET_HEADS, RET_KDIM, RET_VDIM
    qr = rotary(q.astype(F32).reshape(bsz, seq, h, dk), positions)
    kr = rotary(k.astype(F32).reshape(bsz, seq, h, dk), positions) * dk ** -0.5
    qc = qr.reshape(bsz, nc, CHUNK, h, dk)
    kc = kr.reshape(bsz, nc, CHUNK, h, dk)
    vc = v.astype(F32).reshape(bsz, nc, CHUNK, h, dv)
    log_gamma = jnp.log1p(-jnp.exp2(-5.0 - jnp.arange(h, dtype=F32)))
    pos = jnp.arange(CHUNK, dtype=F32)
    d_intra = jnp.exp(log_gamma[:, None, None] * jnp.abs(pos[:, None] - pos[None, :]))
    scores = jnp.einsum('bclhd,bcshd->bchls', qc, kc) * d_intra
    o = jnp.einsum('bchls,bcshv->bclhv', scores, vc)
    to_end = jnp.exp(log_gamma[:, None] * (CHUNK - 1 - pos)).T
    kv = jnp.einsum('bcshd,bcshv->bchdv', kc * to_end[:, :, None], vc)
    chunk_decay = jnp.broadcast_to(jnp.exp(log_gamma * CHUNK)[:, None, None], kv.shape[:3] + (1, 1))
    prev = exclusive_chunk_scan(kv, chunk_decay)
    from_start = jnp.exp(log_gamma[:, None] * (pos + 1.0)).T
    o = o + jnp.einsum('bclhd,bchdv->bclhv', qc, prev) * from_start[:, :, None]
    o = o.reshape(bsz, seq, h, dv)
    mu = jnp.mean(o, -1, keepdims=True)
    var = jnp.mean(jnp.square(o - mu), -1, keepdims=True)
    o = ((o - mu) * lax.rsqrt(var + NORM_EPS)).reshape(bsz, seq, h * dv)
    o = (o * norm_w.astype(F32) + norm_b.astype(F32)) * jax.nn.silu(g.astype(F32))
    return o.astype(g.dtype)


def route(ht, router_w, router_b):
    t = ht.shape[0]
    logits = (ht @ router_w).astype(F32) + router_b.astype(F32)
    probs = jax.nn.softmax(logits, axis=-1)
    grp = probs.reshape(t, N_EXPERT_GROUPS, EXPERTS_PER_GROUP)
    grp_score = jnp.sum(lax.top_k(grp, TOP_K)[0], -1)
    sel_group = jnp.argmax(grp_score, -1)
    in_group = jnp.einsum('tge,tg->te', grp, jax.nn.one_hot(sel_group, N_EXPERT_GROUPS, dtype=F32))
    wts, local = lax.top_k(in_group, TOP_K)
    idx = sel_group[:, None].astype(jnp.int32) * EXPERTS_PER_GROUP + local.astype(jnp.int32)
    return idx, wts / jnp.sum(wts, -1, keepdims=True)


def moe_ffn(h, router_w, router_b, w_gate, w_up, w_down):
    bsz, seq, d = h.shape
    t = bsz * seq
    ht = h.reshape(t, d)
    idx, wts = route(ht, router_w, router_b)
    n_assign = t * TOP_K
    flat_e = idx.reshape(n_assign)
    flat_tok = jnp.arange(n_assign, dtype=jnp.int32) // TOP_K
    flat_w = wts.reshape(n_assign)
    order = jnp.argsort(flat_e)
    se = flat_e[order]
    counts = jnp.bincount(flat_e, length=N_EXPERTS)
    padded = (counts + MOE_BLOCK - 1) // MOE_BLOCK * MOE_BLOCK
    pad_end = jnp.cumsum(padded)
    pad_start = pad_end - padded
    start = jnp.cumsum(counts) - counts
    dest = pad_start[se] + jnp.arange(n_assign, dtype=jnp.int32) - start[se]
    n_blocks = -(-n_assign // MOE_BLOCK) + N_EXPERTS
    n_slots = n_blocks * MOE_BLOCK
    slot_tok = jnp.full((n_slots,), t, jnp.int32).at[dest].set(flat_tok[order])
    slot_w = jnp.zeros((n_slots,), F32).at[dest].set(flat_w[order])
    block_start = jnp.arange(n_blocks, dtype=jnp.int32) * MOE_BLOCK
    block_exp = jnp.minimum(jnp.sum(block_start[:, None] >= pad_end[None, :], -1), N_EXPERTS - 1)
    h_pad = jnp.concatenate([ht, jnp.zeros((1, d), ht.dtype)], 0)

    def block_step(out, inp):
        tok, wt, e = inp
        xb = h_pad[tok]
        hid = jax.nn.silu(xb @ w_gate[e]) * (xb @ w_up[e])
        yb = (hid @ w_down[e]).astype(F32) * wt[:, None]
        return out.at[tok].add(yb), None

    out, _ = lax.scan(block_step, jnp.zeros((t + 1, d), F32),
                      (slot_tok.reshape(n_blocks, MOE_BLOCK), slot_w.reshape(n_blocks, MOE_BLOCK), block_exp))
    return out[:t].reshape(bsz, seq, d).astype(h.dtype)


def setup_inputs(seed: int = 0) -> dict:
    key = jax.random.key(seed)
    ks = jax.random.split(key, 26)
    nrm = jax.random.normal
    x = nrm(ks[0], (BATCH, SEQ, D_MODEL), F32)
    c = nrm(ks[1], (BATCH, D_MODEL), F32)
    positions = (jax.random.randint(ks[2], (BATCH, 1), 0, 4096, dtype=jnp.int32)
                 + jnp.arange(SEQ, dtype=jnp.int32)[None, :])
    w_in = nrm(ks[3], (DEPTH, D_MODEL, IN_WIDTH), F32) * D_MODEL ** -0.5
    conv_w = nrm(ks[4], (DEPTH, SSD_CONV, SSD_CONV_DIM), F32) * SSD_CONV ** -0.5
    conv_b = 0.01 * nrm(ks[5], (DEPTH, SSD_CONV_DIM), F32)
    dt0 = jnp.exp(jax.random.uniform(ks[6], (DEPTH, SSD_HEADS), F32, math.log(1e-3), math.log(1e-1)))
    dt_bias = dt0 + jnp.log(-jnp.expm1(-dt0))
    a_log = jnp.log(jax.random.uniform(ks[7], (DEPTH, SSD_HEADS), F32, 1.0, 16.0))
    d_skip = 1.0 + 0.1 * nrm(ks[8], (DEPTH, SSD_HEADS), F32)
    ssd_norm_w = 1.0 + 0.02 * nrm(ks[9], (DEPTH, SSD_WIDTH), F32)
    hgrn_gamma = 0.1 * nrm(ks[10], (DEPTH, HG_HEADS * HG_KDIM), F32)
    hgrn_norm_w = 1.0 + 0.02 * nrm(ks[11], (DEPTH, HG_WIDTH), F32)
    ret_norm_w = 1.0 + 0.02 * nrm(ks[12], (DEPTH, RET_WIDTH), F32)
    ret_norm_b = 0.02 * nrm(ks[13], (DEPTH, RET_WIDTH), F32)
    w_out = nrm(ks[14], (DEPTH, D_MODEL, D_MODEL), F32) * (D_MODEL ** -0.5 * DEEPNORM_BETA)
    ada_down = nrm(ks[15], (DEPTH, D_MODEL, ADA_RANK), F32) * D_MODEL ** -0.5
    ada_up = nrm(ks[16], (DEPTH, ADA_RANK, N_MOD * D_MODEL), F32) * (0.1 * ADA_RANK ** -0.5)
    ada_b = 0.01 * nrm(ks[17], (DEPTH, N_MOD * D_MODEL), F32)
    ln_g = 1.0 + 0.02 * nrm(ks[18], (DEPTH, 2, D_MODEL), F32)
    ln_b = 0.02 * nrm(ks[19], (DEPTH, 2, D_MODEL), F32)
    router_w = nrm(ks[20], (D_MODEL, N_EXPERTS), F32) * D_MODEL ** -0.5
    router_b = 0.01 * nrm(ks[21], (N_EXPERTS,), F32)
    w_gate = nrm(ks[22], (DEPTH, N_EXPERTS, D_MODEL, D_FF_EXPERT), F32) * D_MODEL ** -0.5
    w_up = nrm(ks[23], (DEPTH, N_EXPERTS, D_MODEL, D_FF_EXPERT), F32) * D_MODEL ** -0.5
    w_down = nrm(ks[24], (DEPTH, N_EXPERTS, D_FF_EXPERT, D_MODEL), F32) * (D_FF_EXPERT ** -0.5 * DEEPNORM_BETA)
    return {'x': x, 'c': c, 'positions': positions, 'w_in': w_in, 'conv_w': conv_w, 'conv_b': conv_b,
            'dt_bias': dt_bias, 'a_log': a_log, 'd_skip': d_skip, 'ssd_norm_w': ssd_norm_w,
            'hgrn_gamma': hgrn_gamma, 'hgrn_norm_w': hgrn_norm_w, 'ret_norm_w': ret_norm_w,
            'ret_norm_b': ret_norm_b, 'w_out': w_out, 'ada_down': ada_down, 'ada_up': ada_up,
            'ada_b': ada_b, 'ln_g': ln_g, 'ln_b': ln_b, 'router_w': router_w, 'router_b': router_b,
            'w_gate': w_gate, 'w_up': w_up, 'w_down': w_down}


def reference(x, c, positions, w_in, conv_w, conv_b, dt_bias, a_log, d_skip, ssd_norm_w,
              hgrn_gamma, hgrn_norm_w, ret_norm_w, ret_norm_b, w_out, ada_down, ada_up, ada_b,
              ln_g, ln_b, router_w, router_b, w_gate, w_up, w_down):
    bsz = x.shape[0]
    p = jax.nn.softmax(hgrn_gamma.astype(F32), axis=0)
    lower_bounds = jnp.cumsum(p, axis=0) - p[0]
    c_act = jax.nn.silu(c)
    for l in range(DEPTH):
        mod = ((c_act @ ada_down[l]) @ ada_up[l] + ada_b[l]).reshape(bsz, N_MOD, D_MODEL)
        shift1, scale1, gate1, shift2, scale2, gate2 = [mod[:, i, None, :] for i in range(N_MOD)]
        h = x * (1.0 + scale1) + shift1
        proj = h @ w_in[l]
        z, xbc, dt_raw, hq, hf, hi, hg, rq, rk, rv, rg = jnp.split(proj, SPLIT_POINTS, axis=-1)
        y_ssd = ssd_mixer(z, xbc, dt_raw, conv_w[l], conv_b[l], dt_bias[l], a_log[l], d_skip[l], ssd_norm_w[l])
        y_hg = hgrn2_mixer(hq, hf, hi, hg, lower_bounds[l], hgrn_norm_w[l])
        y_ret = retention_mixer(rq, rk, rv, rg, positions, ret_norm_w[l], ret_norm_b[l])
        mixed = jnp.concatenate([y_ssd, y_hg, y_ret], axis=-1) @ w_out[l]
        x = layer_norm(DEEPNORM_ALPHA * x + (1.0 + gate1) * mixed, ln_g[l, 0], ln_b[l, 0])
        h = x * (1.0 + scale2) + shift2
        y = moe_ffn(h, router_w, router_b, w_gate[l], w_up[l], w_down[l])
        x = layer_norm(DEEPNORM_ALPHA * x + (1.0 + gate2) * y, ln_g[l, 1], ln_b[l, 1])
    return x
```

```python
import functools
import math

import jax
import jax.numpy as jnp
import numpy as np
from jax import lax
from jax.experimental import pallas as pl
from jax.experimental.pallas import tpu as pltpu

F32 = jnp.float32
BF16 = jnp.bfloat16
I32 = jnp.int32

CHUNK = 64
TINY = 1e-30
NORM_EPS = 1e-5
ROPE_BASE = 10000.0
LANES = 128
SSD_HEAD_DIM = 64
SSD_GROUPS = 4
SSD_STATE = 128
SSD_CONV = 4
HEAD_V = 128
RET_KDIM = 64
N_EXPERTS = 16
N_EXPERT_GROUPS = 4
EXPERTS_PER_GROUP = N_EXPERTS // N_EXPERT_GROUPS
TOP_K = 2
MOE_BLOCK = 256
N_MOD = 6
VMEM_LIMIT = 56 * 1024 * 1024

SSD_L = 128
HG_L = 64
RET_L = 2 * CHUNK


def _cp(sem):
    return pltpu.CompilerParams(dimension_semantics=sem, vmem_limit_bytes=VMEM_LIMIT)


def _pick(n, cands):
    for c in cands:
        if n % c == 0:
            return c
    return n


def _silu(x):
    return x * jax.nn.sigmoid(x)


def _split3(v):
    hi = v.astype(BF16)
    r1 = v - hi.astype(F32)
    mid = r1.astype(BF16)
    lo = (r1 - mid.astype(F32)).astype(BF16)
    return hi, mid, lo


def _dot(a, b):
    return jnp.dot(a, b, preferred_element_type=F32)


def _dot_nt(a, b):
    return lax.dot_general(a, b, (((1,), (1,)), ((), ())), preferred_element_type=F32)


def _sel_right(v, m01):
    hi, mid, lo = _split3(v)
    return _dot(hi, m01) + _dot(mid, m01) + _dot(lo, m01)


def _sel_left(m01, v):
    hi, mid, lo = _split3(v)
    return _dot(m01, hi) + _dot(m01, mid) + _dot(m01, lo)


def _tri(n):
    r = lax.broadcasted_iota(I32, (n, n), 0)
    c = lax.broadcasted_iota(I32, (n, n), 1)
    return r >= c


def _mod_kernel(c_ref, down_ref, up_ref, b_ref, o_ref):
    c = c_ref[...]
    t = jnp.dot(_silu(c), down_ref[...], preferred_element_type=F32,
                precision=lax.Precision.HIGHEST)
    o_ref[...] = jnp.dot(t, up_ref[...], preferred_element_type=F32,
                         precision=lax.Precision.HIGHEST) + b_ref[...]


def _modulation(c, ada_down, ada_up, ada_b):
    depth, d, rank = ada_down.shape
    bsz = c.shape[0]
    rows = 8
    cp = jnp.zeros((rows, d), F32).at[:bsz].set(c)
    out = pl.pallas_call(
        _mod_kernel,
        grid=(depth, N_MOD),
        in_specs=[pl.BlockSpec((rows, d), lambda l, j: (0, 0)),
                  pl.BlockSpec((None, d, rank), lambda l, j: (l, 0, 0)),
                  pl.BlockSpec((None, rank, d), lambda l, j: (l, 0, j)),
                  pl.BlockSpec((None, 1, d), lambda l, j: (l, 0, j))],
        out_specs=pl.BlockSpec((None, None, rows, d), lambda l, j: (l, j, 0, 0)),
        out_shape=jax.ShapeDtypeStruct((depth, N_MOD, rows, d), F32),
        compiler_params=_cp(("arbitrary", "arbitrary")),
    )(cp, ada_down, ada_up, ada_b.reshape(depth, 1, N_MOD * d))
    return out[:, :, :bsz, None, :]


def _modulate_kernel(x_ref, sc_ref, sh_ref, h_ref):
    h_ref[...] = (x_ref[...] * (1.0 + sc_ref[...]) + sh_ref[...]).astype(h_ref.dtype)


def _modulate(x2, scale, shift, seq):
    t, d = x2.shape
    tm = _pick(seq, (512, 256, 128, 64))
    vec = pl.BlockSpec((None, 1, d), lambda i: (i * tm // seq, 0, 0))
    return pl.pallas_call(
        _modulate_kernel,
        grid=(t // tm,),
        in_specs=[pl.BlockSpec((tm, d), lambda i: (i, 0)), vec, vec],
        out_specs=pl.BlockSpec((tm, d), lambda i: (i, 0)),
        out_shape=jax.ShapeDtypeStruct((t, d), BF16),
        compiler_params=_cp(("arbitrary",)),
    )(x2, scale, shift)


def _mm_kernel(a_ref, b_ref, o_ref):
    o_ref[...] = _dot(a_ref[...], b_ref[...]).astype(o_ref.dtype)


def _matmul(a, b, out_dtype):
    m, k = a.shape
    n = b.shape[1]
    tm = _pick(m, (1024, 512, 256, 128))
    tn = _pick(n, (1024, 768, 512, 384, 256, 128))
    return pl.pallas_call(
        _mm_kernel,
        grid=(m // tm, n // tn),
        in_specs=[pl.BlockSpec((tm, k), lambda i, j: (i, 0)),
                  pl.BlockSpec((k, tn), lambda i, j: (0, j))],
        out_specs=pl.BlockSpec((tm, tn), lambda i, j: (i, j)),
        out_shape=jax.ShapeDtypeStruct((m, n), out_dtype),
        compiler_params=_cp(("arbitrary", "arbitrary")),
    )(a, b)


def _mm3_kernel(a1_ref, a2_ref, a3_ref, b1_ref, b2_ref, b3_ref, o_ref):
    o_ref[...] = (_dot(a1_ref[...], b1_ref[...]) + _dot(a2_ref[...], b2_ref[...])
                  + _dot(a3_ref[...], b3_ref[...]))


def _out_proj(y_ssd, y_hg, y_ret, w_out):
    m = y_ssd.shape[0]
    d = w_out.shape[1]
    k1, k2, k3 = y_ssd.shape[1], y_hg.shape[1], y_ret.shape[1]
    tm = _pick(m, (1024, 512, 256, 128))
    tn = _pick(d, (1024, 512, 256, 128))
    return pl.pallas_call(
        _mm3_kernel,
        grid=(m // tm, d // tn),
        in_specs=[pl.BlockSpec((tm, k1), lambda i, j: (i, 0)),
                  pl.BlockSpec((tm, k2), lambda i, j: (i, 0)),
                  pl.BlockSpec((tm, k3), lambda i, j: (i, 0)),
                  pl.BlockSpec((k1, tn), lambda i, j: (0, j)),
                  pl.BlockSpec((k2, tn), lambda i, j: (k1 // k2, j)),
                  pl.BlockSpec((k3, tn), lambda i, j: ((k1 + k2) // k3, j))],
        out_specs=pl.BlockSpec((tm, tn), lambda i, j: (i, j)),
        out_shape=jax.ShapeDtypeStruct((m, d), F32),
        compiler_params=_cp(("arbitrary", "arbitrary")),
    )(y_ssd, y_hg, y_ret, w_out, w_out, w_out)


def _ssd_kernel(z_ref, xs_ref, b_ref, c_ref, dt_ref,
                wx_ref, wb_ref, wc_ref, bx_ref, bb_ref, bc_ref,
                dtb_ref, alog_ref, dskip_ref, nw_ref, e_ref, sel_ref,
                o_ref, st_ref, px_ref, pb_ref, pc_ref, *, heads_per_group):
    L = SSD_L
    ci = pl.program_id(2)

    @pl.when(ci == 0)
    def _():
        st_ref[...] = jnp.zeros_like(st_ref)
        px_ref[0:8, :] = jnp.zeros((8, px_ref.shape[1]), F32)
        pb_ref[0:8, :] = jnp.zeros((8, pb_ref.shape[1]), F32)
        pc_ref[0:8, :] = jnp.zeros((8, pc_ref.shape[1]), F32)

    def conv_silu(raw_ref, pad_ref, w_ref, bias_ref):
        pad_ref[8:8 + L, :] = raw_ref[...]
        acc = bias_ref[...]
        for k in range(SSD_CONV):
            acc = acc + w_ref[k:k + 1, :] * pad_ref[pl.ds(8 - (SSD_CONV - 1) + k, L), :]
        pad_ref[0:8, :] = pad_ref[L:L + 8, :]
        return _silu(acc)

    xs = conv_silu(xs_ref, px_ref, wx_ref, bx_ref)
    bm = conv_silu(b_ref, pb_ref, wb_ref, bb_ref)
    cm = conv_silu(c_ref, pc_ref, wc_ref, bc_ref)

    dt = jax.nn.softplus(dt_ref[...] + dtb_ref[...])
    dta = dt * (-jnp.exp(alog_ref[...]))
    tri = _tri(L)
    cum = _sel_left(tri.astype(BF16), dta)
    cum_g = _sel_right(cum, sel_ref[...])
    cum_gt = cum_g.T
    e01 = e_ref[...]
    dt_full = _sel_right(dt, e01)
    cum_full = _sel_right(cum, e01)
    last = cum_full[L - 1:L, :]
    to_end = jnp.exp(last - cum_full)
    from_start = jnp.exp(cum_full)
    xdt = xs * dt_full

    bm16 = bm.astype(BF16)
    cm16 = cm.astype(BF16)
    cb = _dot_nt(cm16, bm16)
    lane = lax.broadcasted_iota(I32, (L, LANES), 1)
    lo_half = lane < SSD_HEAD_DIM
    ys = []
    for j in range(heads_per_group // 2):
        atts = []
        for i in (2 * j, 2 * j + 1):
            dec = jnp.where(tri, jnp.exp(jnp.minimum(cum_g[:, i:i + 1] - cum_gt[i:i + 1, :], 0.0)), 0.0)
            atts.append((cb * dec).astype(BF16))
        xp = xdt[:, j * LANES:(j + 1) * LANES]
        stack = jnp.concatenate([jnp.where(lo_half, xp, 0.0), jnp.where(lo_half, 0.0, xp)], axis=0)
        ys.append(_dot(jnp.concatenate(atts, axis=1), stack.astype(BF16)))
    y = jnp.concatenate(ys, axis=1) if len(ys) > 1 else ys[0]

    st = st_ref[...]
    y = y + _dot(cm16, st.astype(BF16)) * from_start
    st_ref[...] = st * jnp.exp(last) + _dot(bm.T.astype(BF16), (xdt * to_end).astype(BF16))

    y = y + xs * dskip_ref[...]
    y = y * _silu(z_ref[...])
    y = y * lax.rsqrt(jnp.mean(jnp.square(y), axis=-1, keepdims=True) + NORM_EPS) * nw_ref[...]
    o_ref[...] = y.astype(o_ref.dtype)


def _ssd(proj, dt_raw, conv_w, conv_b, dt_bias, a_log, d_skip, norm_w, bsz, seq, ssd_w):
    t = proj.shape[0]
    L = SSD_L
    nc = seq // L
    g = SSD_GROUPS
    gw = ssd_w // g
    n = SSD_STATE
    heads = ssd_w // SSD_HEAD_DIM
    r = heads // g
    assert r % 2 == 0 and gw % LANES == 0 and seq % L == 0 and heads <= LANES
    off_b = 2 * ssd_w
    off_c = off_b + g * n
    hpad = LANES - heads
    dtb = jnp.pad(dt_bias, (0, hpad)).reshape(1, LANES)
    alog = jnp.pad(a_log, (0, hpad)).reshape(1, LANES)
    dskip = jnp.repeat(d_skip, SSD_HEAD_DIM).reshape(1, ssd_w)
    hid = jnp.arange(LANES)
    e01 = (hid[:, None] == (jnp.arange(ssd_w) // SSD_HEAD_DIM)[None, :]).astype(BF16)
    sel = (hid[None, :, None] == (jnp.arange(g)[:, None, None] * r + hid[None, None, :])
           ) & (hid[None, None, :] < r)
    sel = sel.astype(BF16)
    row = lambda b, gi, c: b * nc + c
    cw = conv_w
    cb2 = conv_b.reshape(1, -1)
    in_specs = [
        pl.BlockSpec((L, gw), lambda b, gi, c: (row(b, gi, c), gi)),
        pl.BlockSpec((L, gw), lambda b, gi, c: (row(b, gi, c), g + gi)),
        pl.BlockSpec((L, n), lambda b, gi, c: (row(b, gi, c), off_b // n + gi)),
        pl.BlockSpec((L, n), lambda b, gi, c: (row(b, gi, c), off_c // n + gi)),
        pl.BlockSpec((L, LANES), lambda b, gi, c: (row(b, gi, c), 0)),
        pl.BlockSpec((SSD_CONV, gw), lambda b, gi, c: (0, gi)),
        pl.BlockSpec((SSD_CONV, n), lambda b, gi, c: (0, ssd_w // n + gi)),
        pl.BlockSpec((SSD_CONV, n), lambda b, gi, c: (0, (ssd_w + g * n) // n + gi)),
        pl.BlockSpec((1, gw), lambda b, gi, c: (0, gi)),
        pl.BlockSpec((1, n), lambda b, gi, c: (0, ssd_w // n + gi)),
        pl.BlockSpec((1, n), lambda b, gi, c: (0, (ssd_w + g * n) // n + gi)),
        pl.BlockSpec((1, LANES), lambda b, gi, c: (0, 0)),
        pl.BlockSpec((1, LANES), lambda b, gi, c: (0, 0)),
        pl.BlockSpec((1, gw), lambda b, gi, c: (0, gi)),
        pl.BlockSpec((1, gw), lambda b, gi, c: (0, gi)),
        pl.BlockSpec((LANES, gw), lambda b, gi, c: (0, gi)),
        pl.BlockSpec((None, LANES, LANES), lambda b, gi, c: (gi, 0, 0)),
    ]
    return pl.pallas_call(
        functools.partial(_ssd_kernel, heads_per_group=r),
        grid=(bsz, g, nc),
        in_specs=in_specs,
        out_specs=pl.BlockSpec((L, gw), lambda b, gi, c: (row(b, gi, c), gi)),
        out_shape=jax.ShapeDtypeStruct((t, ssd_w), BF16),
        scratch_shapes=[pltpu.VMEM((n, gw), F32),
                        pltpu.VMEM((L + 8, gw), F32),
                        pltpu.VMEM((L + 8, n), F32),
                        pltpu.VMEM((L + 8, n), F32)],
        compiler_params=_cp(("arbitrary", "arbitrary", "arbitrary")),
    )(proj, proj, proj, proj, dt_raw, cw, cw, cw, cb2, cb2, cb2,
      dtb, alog, dskip, norm_w.reshape(1, ssd_w), e01, sel)


def _hgrn_kernel(q_ref, f_ref, v_ref, g_ref, lb_ref, nw_ref, o_ref,
                 st_ref, key_ref, bc_ref, p_ref, r_ref):
    L = HG_L
    dk = LANES
    ci = pl.program_id(2)

    @pl.when(ci == 0)
    def _():
        st_ref[...] = jnp.zeros_like(st_ref)

    q = q_ref[...]
    fr = f_ref[...]
    lb = lb_ref[...]
    log_f = jnp.log(jnp.maximum(lb + (1.0 - lb) * jax.nn.sigmoid(fr), TINY))
    key = (1.0 - lb) * jax.nn.sigmoid(-fr)
    bcum = _sel_left(_tri(L).astype(BF16), log_f)
    key_ref[...] = key
    bc_ref[...] = bcum
    last = bcum[L - 1:L, :]

    def build(j, carry):
        kj = key_ref[pl.ds(j, 1), :]
        bj = bc_ref[pl.ds(j, 1), :]
        p = q * kj * jnp.exp(jnp.minimum(bcum - bj, 0.0))
        p_ref[pl.ds(pl.multiple_of(j * L, L), L), :] = p.astype(BF16)
        return carry
    lax.fori_loop(0, L, build, 0, unroll=4)
    r_ref[...] = _dot(p_ref[...], jnp.ones((dk, LANES), BF16))
    lane = lax.broadcasted_iota(I32, (L, LANES), 1)

    def place(j, att):
        return jnp.where(lane == j, r_ref[pl.ds(pl.multiple_of(j * L, L), L), :], att)
    att = lax.fori_loop(0, L, place, jnp.zeros((L, LANES), F32), unroll=4)
    rowi = lax.broadcasted_iota(I32, (L, LANES), 0)
    att = jnp.where(rowi >= lane, att, 0.0)

    zpad = jnp.zeros((LANES - L, LANES), F32)
    v = v_ref[...]
    vpad = jnp.concatenate([v, zpad], axis=0).astype(BF16)
    st = st_ref[...]
    o = _dot_nt((q * jnp.exp(bcum)).astype(BF16), st.astype(BF16)) + _dot(att.astype(BF16), vpad)
    kk = jnp.concatenate([key * jnp.exp(last - bcum), zpad], axis=0)
    vt = jnp.concatenate([v, zpad], axis=0).T
    st_ref[...] = st * jnp.exp(last) + _dot(vt.astype(BF16), kk.astype(BF16))

    o = o * lax.rsqrt(jnp.mean(jnp.square(o), axis=-1, keepdims=True) + NORM_EPS) * nw_ref[...]
    o_ref[...] = (o * _silu(g_ref[...])).astype(o_ref.dtype)


def _hgrn(proj, lower_bound, norm_w, bsz, seq, hg_w, off_q):
    t = proj.shape[0]
    L = HG_L
    nc = seq // L
    h = hg_w // HEAD_V
    cq = off_q // LANES
    row = lambda b, hi, c: b * nc + c
    blk = lambda k: pl.BlockSpec((L, LANES), lambda b, hi, c: (row(b, hi, c), cq + k * h + hi))
    vec = pl.BlockSpec((1, LANES), lambda b, hi, c: (0, hi))
    return pl.pallas_call(
        _hgrn_kernel,
        grid=(bsz, h, nc),
        in_specs=[blk(0), blk(1), blk(2), blk(3), vec, vec],
        out_specs=pl.BlockSpec((L, LANES), lambda b, hi, c: (row(b, hi, c), hi)),
        out_shape=jax.ShapeDtypeStruct((t, hg_w), BF16),
        scratch_shapes=[pltpu.VMEM((HEAD_V, LANES), F32),
                        pltpu.VMEM((L, LANES), F32),
                        pltpu.VMEM((L, LANES), F32),
                        pltpu.VMEM((L * L, LANES), BF16),
                        pltpu.VMEM((L * L, LANES), F32)],
        compiler_params=_cp(("arbitrary", "arbitrary", "arbitrary")),
    )(proj, proj, proj, proj, lower_bound.reshape(1, hg_w), norm_w.reshape(1, hg_w))


def _rope_kernel(pos_ref, freq_ref, cos_ref, sin_ref):
    ang = pos_ref[...].astype(F32) * freq_ref[...]
    lane = lax.broadcasted_iota(I32, ang.shape, 1)
    first = (lane % RET_KDIM) < (RET_KDIM // 2)
    cos_ref[...] = jnp.cos(ang)
    s = jnp.sin(ang)
    sin_ref[...] = jnp.where(first, -s, s)


def _rope_tables(positions):
    t = positions.size
    half = RET_KDIM // 2
    freq = ROPE_BASE ** (-jnp.arange(half, dtype=F32) / half)
    freq = jnp.tile(freq, LANES // half).reshape(1, LANES)
    tm = _pick(t, (512, 256, 128))
    return pl.pallas_call(
        _rope_kernel,
        grid=(t // tm,),
        in_specs=[pl.BlockSpec((tm, 1), lambda i: (i, 0)),
                  pl.BlockSpec((1, LANES), lambda i: (0, 0))],
        out_specs=[pl.BlockSpec((tm, LANES), lambda i: (i, 0))] * 2,
        out_shape=[jax.ShapeDtypeStruct((t, LANES), F32)] * 2,
        compiler_params=_cp(("arbitrary",)),
    )(positions.reshape(t, 1), freq)


def _ret_kernel(q_ref, k_ref, v_ref, g_ref, cos_ref, sin_ref, dm_ref, te_ref, fs_ref, cd_ref,
                nw_ref, nb_ref, o_ref, st_ref):
    L = RET_L
    ci = pl.program_id(2)

    @pl.when(ci == 0)
    def _():
        st_ref[...] = jnp.zeros_like(st_ref)

    cos = cos_ref[...]
    sin = sin_ref[...]
    lane = lax.broadcasted_iota(I32, (L, LANES), 1)
    rowi = lax.broadcasted_iota(I32, (L, LANES), 0)
    first = (lane % RET_KDIM) < (RET_KDIM // 2)

    def rope(x):
        sw = jnp.where(first, pltpu.roll(x, LANES - RET_KDIM // 2, 1), pltpu.roll(x, RET_KDIM // 2, 1))
        return x * cos + sw * sin

    qr = rope(q_ref[...])
    kr = rope(k_ref[...]) * RET_KDIM ** -0.5
    kr16 = kr.astype(BF16)
    v = v_ref[...]
    v16 = v.astype(BF16)

    kt = (kr * te_ref[...]).T
    cd = cd_ref[...]

    def summary(frames):
        full = _dot(jnp.where(frames, kt, 0.0).astype(BF16), v16)
        return jnp.where(rowi < RET_KDIM, full[:, :HEAD_V], full[:, HEAD_V:])
    s0 = st_ref[...]
    s1 = s0 * cd + summary(lane < CHUNK)
    st_ref[...] = s1 * cd + summary(lane >= CHUNK)
    s0_16 = s0.astype(BF16)
    s1_16 = s1.astype(BF16)

    outs = []
    for hh in range(2):
        qh = jnp.where((lane // RET_KDIM) == hh, qr, 0.0).astype(BF16)
        scores = _dot_nt(qh, kr16) * dm_ref[hh]
        o = _dot(scores.astype(BF16), v16[:, hh * HEAD_V:(hh + 1) * HEAD_V])
        inter = jnp.where(rowi < CHUNK, _dot(qh, s0_16), _dot(qh, s1_16))
        o = o + inter * fs_ref[hh]
        mu = jnp.mean(o, axis=-1, keepdims=True)
        var = jnp.mean(jnp.square(o - mu), axis=-1, keepdims=True)
        sl = slice(hh * HEAD_V, (hh + 1) * HEAD_V)
        o = (o - mu) * lax.rsqrt(var + NORM_EPS) * nw_ref[:, sl] + nb_ref[:, sl]
        outs.append(o * _silu(g_ref[:, sl]))
    o_ref[...] = jnp.concatenate(outs, axis=1).astype(o_ref.dtype)


def _retention(proj, cos, sin, norm_w, norm_b, bsz, seq, ret_w, off_q):
    t = proj.shape[0]
    L = RET_L
    nc = seq // L
    h = ret_w // HEAD_V
    pairs = h // 2
    assert h % 2 == 0 and seq % L == 0
    kw = h * RET_KDIM
    off_k = off_q + kw
    off_v = off_k + kw
    off_g = off_v + ret_w
    assert off_v % (2 * HEAD_V) == 0 and off_g % (2 * HEAD_V) == 0
    log_gamma = jnp.log1p(-jnp.exp2(-5.0 - jnp.arange(h, dtype=F32)))
    pos = jnp.arange(CHUNK, dtype=F32)
    d_intra = jnp.exp(log_gamma[:, None, None] * jnp.abs(pos[:, None] - pos[None, :]))
    dm = jnp.zeros((h, L, L), F32)
    dm = dm.at[:, :CHUNK, :CHUNK].set(d_intra).at[:, CHUNK:, CHUNK:].set(d_intra)
    to_end = jnp.exp(log_gamma[:, None] * (CHUNK - 1 - pos))
    te = jnp.repeat(jnp.tile(to_end, (1, 2)).reshape(pairs, 2, L), RET_KDIM, axis=1)
    te = jnp.swapaxes(te, 1, 2)
    from_start = jnp.exp(log_gamma[:, None] * (pos + 1.0))
    fs = jnp.broadcast_to(jnp.tile(from_start, (1, 2))[:, :, None], (h, L, HEAD_V))
    cdec = jnp.exp(log_gamma * CHUNK)
    cd = jnp.broadcast_to(jnp.repeat(cdec.reshape(pairs, 2), RET_KDIM, axis=1)[:, :, None],
                          (pairs, 2 * RET_KDIM, HEAD_V))
    row = lambda b, p, c: b * nc + c
    tab = lambda n0: pl.BlockSpec((n0, L, LANES), lambda b, p, c: (p, 0, 0))
    return pl.pallas_call(
        _ret_kernel,
        grid=(bsz, pairs, nc),
        in_specs=[pl.BlockSpec((L, LANES), lambda b, p, c: (row(b, p, c), off_q // LANES + p)),
                  pl.BlockSpec((L, LANES), lambda b, p, c: (row(b, p, c), off_k // LANES + p)),
                  pl.BlockSpec((L, 2 * HEAD_V), lambda b, p, c: (row(b, p, c), off_v // (2 * HEAD_V) + p)),
                  pl.BlockSpec((L, 2 * HEAD_V), lambda b, p, c: (row(b, p, c), off_g // (2 * HEAD_V) + p)),
                  pl.BlockSpec((L, LANES), lambda b, p, c: (row(b, p, c), 0)),
                  pl.BlockSpec((L, LANES), lambda b, p, c: (row(b, p, c), 0)),
                  tab(2),
                  pl.BlockSpec((None, L, LANES), lambda b, p, c: (p, 0, 0)),
                  tab(2),
                  pl.BlockSpec((None, 2 * RET_KDIM, HEAD_V), lambda b, p, c: (p, 0, 0)),
                  pl.BlockSpec((1, 2 * HEAD_V), lambda b, p, c: (0, p)),
                  pl.BlockSpec((1, 2 * HEAD_V), lambda b, p, c: (0, p))],
        out_specs=pl.BlockSpec((L, 2 * HEAD_V), lambda b, p, c: (row(b, p, c), p)),
        out_shape=jax.ShapeDtypeStruct((t, ret_w), BF16),
        scratch_shapes=[pltpu.VMEM((2 * RET_KDIM, HEAD_V), F32)],
        compiler_params=_cp(("arbitrary", "arbitrary", "arbitrary")),
    )(proj, proj, proj, proj, cos, sin, dm, te, fs, cd,
      norm_w.reshape(1, ret_w), norm_b.reshape(1, ret_w))


def _layer_norm_rows(v, g, b):
    mu = jnp.mean(v, axis=-1, keepdims=True)
    var = jnp.mean(jnp.square(v - mu), axis=-1, keepdims=True)
    return (v - mu) * lax.rsqrt(var + NORM_EPS) * g + b


def _route_rows(p):
    rows = [p[e:e + 1, :] for e in range(N_EXPERTS)]
    scores = []
    for gi in range(N_EXPERT_GROUPS):
        a, b, c, d = rows[gi * EXPERTS_PER_GROUP:(gi + 1) * EXPERTS_PER_GROUP]
        hi1, lo1 = jnp.maximum(a, b), jnp.minimum(a, b)
        hi2, lo2 = jnp.maximum(c, d), jnp.minimum(c, d)
        scores.append(jnp.maximum(hi1, hi2) + jnp.maximum(jnp.minimum(hi1, hi2), jnp.maximum(lo1, lo2)))
    best = scores[0]
    sel = jnp.zeros_like(best, dtype=I32)
    for gi in range(1, N_EXPERT_GROUPS):
        better = scores[gi] > best
        best = jnp.where(better, scores[gi], best)
        sel = jnp.where(better, gi, sel)
    cand = []
    for e in range(EXPERTS_PER_GROUP):
        v = rows[e]
        for gi in range(1, N_EXPERT_GROUPS):
            v = jnp.where(sel == gi, rows[gi * EXPERTS_PER_GROUP + e], v)
        cand.append(v)

    def argbest(vals):
        bv = vals[0]
        bi = jnp.zeros_like(sel)
        for e in range(1, len(vals)):
            better = vals[e] > bv
            bv = jnp.where(better, vals[e], bv)
            bi = jnp.where(better, e, bi)
        return bv, bi
    w0, i0 = argbest(cand)
    w1, i1 = argbest([jnp.where(i0 == e, -1.0, cand[e]) for e in range(EXPERTS_PER_GROUP)])
    tot = w0 + w1
    base = sel * EXPERTS_PER_GROUP
    return base + i0, base + i1, w0 / tot, w1 / tot


def _ln_kernel(*refs, alpha, pair_rows, emit_h, h_dtype, with_router):
    it = iter(refs)
    x_ref, y_ref = next(it), next(it)
    if pair_rows:
        y1_ref = next(it)
    gate_ref, g_ref, b_ref = next(it), next(it), next(it)
    if emit_h:
        sc_ref, sh_ref = next(it), next(it)
    if with_router:
        rwh_ref, rwl_ref, rb_ref = next(it), next(it), next(it)
    xo_ref = next(it)
    if emit_h:
        h_ref = next(it)
    if with_router:
        idx_ref, wt_ref = next(it), next(it)
    tm = x_ref.shape[0]
    y = y_ref[...]
    if pair_rows:
        y = y + y1_ref[...]
    xn = _layer_norm_rows(alpha * x_ref[...] + (1.0 + gate_ref[...]) * y, g_ref[...], b_ref[...])
    xo_ref[...] = xn
    if emit_h:
        h = xn * (1.0 + sc_ref[...]) + sh_ref[...]
        h_ref[...] = h.astype(h_dtype)
    if with_router:
        h_hi = h.astype(BF16)
        h_lo = (h - h_hi.astype(F32)).astype(BF16)
        rwh = rwh_ref[...]
        logits = (_dot_nt(rwh, h_hi) + _dot_nt(rwh, h_lo) + _dot_nt(rwl_ref[...], h_hi)
                  + rb_ref[...])
        e = jnp.exp(logits - jnp.max(logits, axis=0, keepdims=True))
        p = e / jnp.sum(e, axis=0, keepdims=True)
        i0, i1, w0, w1 = _route_rows(p)
        zi = jnp.zeros((6, tm), I32)
        idx_ref[...] = jnp.concatenate([i0, i1, zi], axis=0)
        wt_ref[...] = jnp.concatenate([w0, w1, zi.astype(F32)], axis=0)


def _post_norm(x2, y, gate, ln_g, ln_b, alpha, seq, *, second_rows=None, scale=None, shift=None,
               h_dtype=BF16, router=None):
    t, d = x2.shape
    tm = _pick(seq, (128, 64))
    emit_h = scale is not None
    with_router = router is not None
    pair_rows = second_rows is not None
    vec = pl.BlockSpec((None, 1, d), lambda i: (i * tm // seq, 0, 0))
    par = pl.BlockSpec((1, d), lambda i: (0, 0))
    rows = pl.BlockSpec((tm, d), lambda i: (i, 0))
    in_specs = [rows, rows]
    args = [x2, y]
    if pair_rows:
        assert second_rows % tm == 0
        in_specs.append(pl.BlockSpec((tm, d), lambda i: (second_rows // tm + i, 0)))
        args.append(y)
    in_specs += [vec, par, par]
    args += [gate, ln_g.reshape(1, d), ln_b.reshape(1, d)]
    out_specs = [rows]
    out_shape = [jax.ShapeDtypeStruct((t, d), F32)]
    if emit_h:
        in_specs += [vec, vec]
        args += [scale, shift]
        out_specs.append(rows)
        out_shape.append(jax.ShapeDtypeStruct((t, d), h_dtype))
    if with_router:
        rwh, rwl, rb = router
        e = rwh.shape[0]
        in_specs += [pl.BlockSpec((e, d), lambda i: (0, 0)), pl.BlockSpec((e, d), lambda i: (0, 0)),
                     pl.BlockSpec((e, 1), lambda i: (0, 0))]
        args += [rwh, rwl, rb]
        out_specs += [pl.BlockSpec((8, tm), lambda i: (0, i))] * 2
        out_shape += [jax.ShapeDtypeStruct((8, t), I32), jax.ShapeDtypeStruct((8, t), F32)]
    return pl.pallas_call(
        functools.partial(_ln_kernel, alpha=alpha, pair_rows=pair_rows, emit_h=emit_h,
                          h_dtype=h_dtype, with_router=with_router),
        grid=(t // tm,),
        in_specs=in_specs, out_specs=out_specs, out_shape=out_shape,
        compiler_params=_cp(("arbitrary",)),
    )(*args)


def _moe_kernel(src_ref, dst_ref, exp_ref, h_hbm, wg_ref, wu_ref, wd_ref, sw_ref, y_hbm,
                xbuf, ybuf, gsem, ssem):
    i = pl.program_id(0)
    nb = pl.num_programs(0)
    R = MOE_BLOCK
    slot = i % 2

    def gather(block, s):
        def body(r, c):
            tok = src_ref[block * R + r]
            pltpu.make_async_copy(h_hbm.at[pl.ds(tok, 1), :], xbuf.at[s, pl.ds(r, 1), :], gsem.at[s]).start()
            return c
        lax.fori_loop(0, R, body, 0)

    def wait_gather(s):
        pltpu.make_async_copy(h_hbm.at[pl.ds(0, R), :], xbuf.at[s], gsem.at[s]).wait()

    def wait_scatter(s):
        pltpu.make_async_copy(ybuf.at[s], y_hbm.at[pl.ds(0, R), :], ssem.at[s]).wait()

    @pl.when(i == 0)
    def _():
        gather(0, 0)

    @pl.when(i + 1 < nb)
    def _():
        gather(i + 1, 1 - slot)

    wait_gather(slot)
    x = xbuf[slot].astype(BF16)
    hid = _silu(_dot(x, wg_ref[...])) * _dot(x, wu_ref[...])
    yb = _dot(hid.astype(BF16), wd_ref[...]) * sw_ref[...]

    @pl.when(i >= 2)
    def _():
        wait_scatter(slot)
    ybuf[slot] = yb

    def scatter(r, c):
        row = dst_ref[i * R + r]
        pltpu.make_async_copy(ybuf.at[slot, pl.ds(r, 1), :], y_hbm.at[pl.ds(row, 1), :], ssem.at[slot]).start()
        return c
    lax.fori_loop(0, R, scatter, 0)

    @pl.when(i == nb - 1)
    def _():
        wait_scatter(slot)
        wait_scatter(1 - slot)


def _moe_blocks(h2, slot_src, slot_dst, block_exp, slot_w, wg, wu, wd, n_rows_out):
    t, d = h2.shape
    f = wg.shape[2]
    n_blocks = block_exp.shape[0]
    R = MOE_BLOCK
    grid_spec = pltpu.PrefetchScalarGridSpec(
        num_scalar_prefetch=3,
        grid=(n_blocks,),
        in_specs=[pl.BlockSpec(memory_space=pl.ANY),
                  pl.BlockSpec((None, d, f), lambda i, s, dd, e: (e[i], 0, 0)),
                  pl.BlockSpec((None, d, f), lambda i, s, dd, e: (e[i], 0, 0)),
                  pl.BlockSpec((None, f, d), lambda i, s, dd, e: (e[i], 0, 0)),
                  pl.BlockSpec((R, 1), lambda i, s, dd, e: (i, 0))],
        out_specs=pl.BlockSpec(memory_space=pl.ANY),
        scratch_shapes=[pltpu.VMEM((2, R, d), F32),
                        pltpu.VMEM((2, R, d), F32),
                        pltpu.SemaphoreType.DMA((2,)),
                        pltpu.SemaphoreType.DMA((2,))],
    )
    return pl.pallas_call(
        _moe_kernel,
        grid_spec=grid_spec,
        out_shape=jax.ShapeDtypeStruct((n_rows_out, d), F32),
        compiler_params=_cp(("arbitrary",)),
    )(slot_src, slot_dst, block_exp, h2, wg, wu, wd, slot_w)


def _dispatch_plan(idx, wts, t):
    n_assign = t * TOP_K
    flat_e = idx[:TOP_K].T.reshape(n_assign)
    flat_w = wts[:TOP_K].T.reshape(n_assign)
    onehot = (flat_e[:, None] == jnp.arange(N_EXPERTS, dtype=I32)[None, :]).astype(I32)
    csum = jnp.cumsum(onehot, axis=0)
    counts = csum[-1]
    rank = jnp.sum(csum * onehot, axis=1) - 1
    padded = (counts + MOE_BLOCK - 1) // MOE_BLOCK * MOE_BLOCK
    pad_end = jnp.cumsum(padded)
    pad_start = pad_end - padded
    dest = pad_start[flat_e] + rank
    n_blocks = -(-n_assign // MOE_BLOCK) + N_EXPERTS
    n_slots = n_blocks * MOE_BLOCK
    block_start = jnp.arange(n_blocks, dtype=I32) * MOE_BLOCK
    block_exp = jnp.minimum(jnp.sum(block_start[:, None] >= pad_end[None, :], -1), N_EXPERTS - 1).astype(I32)
    a = jnp.arange(n_assign, dtype=I32)
    s = jnp.arange(n_slots, dtype=I32)
    n_spare = n_slots - n_assign
    second = t + n_spare
    is_pad = jnp.ones((n_slots,), I32).at[dest].set(0)
    spare = t + jnp.cumsum(is_pad) - 1
    slot_src = jnp.zeros((n_slots,), I32).at[dest].set(a // TOP_K)
    slot_dst = spare.at[dest].set((a % TOP_K) * second + a // TOP_K)
    slot_w = jnp.zeros((n_slots,), F32).at[dest].set(flat_w)
    return slot_src, slot_dst, block_exp, slot_w.reshape(n_slots, 1), second


def kernel(x, c, positions, w_in, conv_w, conv_b, dt_bias, a_log, d_skip, ssd_norm_w, hgrn_gamma,
           hgrn_norm_w, ret_norm_w, ret_norm_b, w_out, ada_down, ada_up, ada_b, ln_g, ln_b,
           router_w, router_b, w_gate, w_up, w_down):
    bsz, seq, d = x.shape
    depth = w_in.shape[0]
    t = bsz * seq
    ssd_w = d // 2
    ssd_heads = ssd_w // SSD_HEAD_DIM
    conv_dim = ssd_w + 2 * SSD_GROUPS * SSD_STATE
    hg_w = d // 4
    ret_w = d // 4
    ret_kw = (ret_w // HEAD_V) * RET_KDIM
    alpha = (2.0 * depth) ** 0.25
    off_dt = ssd_w + conv_dim
    off_hq = off_dt
    off_rq = off_hq + 4 * hg_w

    p = jax.nn.softmax(hgrn_gamma.astype(F32), axis=0)
    lower_bounds = jnp.cumsum(p, axis=0) - p[0]
    mod = _modulation(c, ada_down, ada_up, ada_b)
    cos, sin = _rope_tables(positions)

    rwt = router_w.T
    rw_hi = rwt.astype(BF16)
    rw_lo = (rwt - rw_hi.astype(F32)).astype(BF16)
    rb = router_b.reshape(N_EXPERTS, 1).astype(F32)

    x2 = x.reshape(t, d)
    h = _modulate(x2, mod[0, 1], mod[0, 0], seq)
    for l in range(depth):
        shift1, scale1, gate1, shift2, scale2, gate2 = [mod[l, i] for i in range(N_MOD)]
        w_main = jnp.concatenate([w_in[l][:, :off_dt], w_in[l][:, off_dt + ssd_heads:]], axis=1).astype(BF16)
        w_dt = jnp.pad(w_in[l][:, off_dt:off_dt + ssd_heads], ((0, 0), (0, LANES - ssd_heads))).astype(BF16)
        proj = _matmul(h, w_main, F32)
        dt_raw = _matmul(h, w_dt, F32)
        y_ssd = _ssd(proj, dt_raw, conv_w[l], conv_b[l], dt_bias[l], a_log[l], d_skip[l], ssd_norm_w[l],
                     bsz, seq, ssd_w)
        y_hg = _hgrn(proj, lower_bounds[l], hgrn_norm_w[l], bsz, seq, hg_w, off_hq)
        y_ret = _retention(proj, cos, sin, ret_norm_w[l], ret_norm_b[l], bsz, seq, ret_w, off_rq)
        mixed = _out_proj(y_ssd, y_hg, y_ret, w_out[l].astype(BF16))
        x2, h2, idx, wts = _post_norm(x2, mixed, gate1, ln_g[l, 0], ln_b[l, 0], alpha, seq,
                                      scale=scale2, shift=shift2, h_dtype=F32, router=(rw_hi, rw_lo, rb))
        slot_src, slot_dst, block_exp, slot_w, second = _dispatch_plan(idx, wts, t)
        y2 = _moe_blocks(h2, slot_src, slot_dst, block_exp, slot_w,
                         w_gate[l].astype(BF16), w_up[l].astype(BF16), w_down[l].astype(BF16), second + t)
        if l + 1 < depth:
            x2, h = _post_norm(x2, y2, gate2, ln_g[l, 1], ln_b[l, 1], alpha, seq, second_rows=second,
                               scale=mod[l + 1, 1], shift=mod[l + 1, 0])
        else:
            (x2,) = _post_norm(x2, y2, gate2, ln_g[l, 1], ln_b[l, 1], alpha, seq, second_rows=second)
    return x2.reshape(bsz, seq, d)
```

```python
import functools
import math

import jax
import jax.numpy as jnp
import numpy as np
from jax import lax
from jax.experimental import pallas as pl
from jax.experimental.pallas import tpu as pltpu

F32 = jnp.float32
BF16 = jnp.bfloat16
I32 = jnp.int32

CHUNK = 64
TINY = 1e-30
NORM_EPS = 1e-5
ROPE_BASE = 10000.0
LANES = 128
SSD_HEAD_DIM = 64
SSD_GROUPS = 4
SSD_STATE = 128
SSD_CONV = 4
HEAD_V = 128
RET_KDIM = 64
N_EXPERTS = 16
N_EXPERT_GROUPS = 4
EXPERTS_PER_GROUP = N_EXPERTS // N_EXPERT_GROUPS
TOP_K = 2
MOE_BLOCK = 256
N_MOD = 6
VMEM_LIMIT = 56 * 1024 * 1024

SSD_L = 128
HG_L = 64
HG_SUB = 16
RET_L = 2 * CHUNK


def _cp(sem):
    return pltpu.CompilerParams(dimension_semantics=sem, vmem_limit_bytes=VMEM_LIMIT)


def _pick(n, cands):
    for c in cands:
        if n % c == 0:
            return c
    return n


def _silu(x):
    return x * jax.nn.sigmoid(x)


def _split3(v):
    hi = v.astype(BF16)
    r1 = v - hi.astype(F32)
    mid = r1.astype(BF16)
    lo = (r1 - mid.astype(F32)).astype(BF16)
    return hi, mid, lo


def _dot(a, b):
    return jnp.dot(a, b, preferred_element_type=F32)


def _dot_nt(a, b):
    return lax.dot_general(a, b, (((1,), (1,)), ((), ())), preferred_element_type=F32)


def _sel_right(v, m01):
    hi, mid, lo = _split3(v)
    return _dot(hi, m01) + _dot(mid, m01) + _dot(lo, m01)


def _sel_left(m01, v):
    hi, mid, lo = _split3(v)
    return _dot(m01, hi) + _dot(m01, mid) + _dot(m01, lo)


def _tri(n):
    r = lax.broadcasted_iota(I32, (n, n), 0)
    c = lax.broadcasted_iota(I32, (n, n), 1)
    return r >= c


def _mod_kernel(c_ref, down_ref, up_ref, b_ref, o_ref):
    c = c_ref[...]
    t = jnp.dot(_silu(c), down_ref[...], preferred_element_type=F32,
                precision=lax.Precision.HIGHEST)
    o_ref[...] = jnp.dot(t, up_ref[...], preferred_element_type=F32,
                         precision=lax.Precision.HIGHEST) + b_ref[...]


def _modulation(c, ada_down, ada_up, ada_b):
    depth, d, rank = ada_down.shape
    bsz = c.shape[0]
    rows = 8
    cp = jnp.zeros((rows, d), F32).at[:bsz].set(c)
    out = pl.pallas_call(
        _mod_kernel,
        grid=(depth, N_MOD),
        in_specs=[pl.BlockSpec((rows, d), lambda l, j: (0, 0)),
                  pl.BlockSpec((None, d, rank), lambda l, j: (l, 0, 0)),
                  pl.BlockSpec((None, rank, d), lambda l, j: (l, 0, j)),
                  pl.BlockSpec((None, 1, d), lambda l, j: (l, 0, j))],
        out_specs=pl.BlockSpec((None, None, rows, d), lambda l, j: (l, j, 0, 0)),
        out_shape=jax.ShapeDtypeStruct((depth, N_MOD, rows, d), F32),
        compiler_params=_cp(("arbitrary", "arbitrary")),
    )(cp, ada_down, ada_up, ada_b.reshape(depth, 1, N_MOD * d))
    return out[:, :, :bsz, None, :]


def _modulate_kernel(x_ref, sc_ref, sh_ref, h_ref):
    h_ref[...] = (x_ref[...] * (1.0 + sc_ref[...]) + sh_ref[...]).astype(h_ref.dtype)


def _modulate(x2, scale, shift, seq):
    t, d = x2.shape
    tm = _pick(seq, (512, 256, 128, 64))
    vec = pl.BlockSpec((None, 1, d), lambda i: (i * tm // seq, 0, 0))
    return pl.pallas_call(
        _modulate_kernel,
        grid=(t // tm,),
        in_specs=[pl.BlockSpec((tm, d), lambda i: (i, 0)), vec, vec],
        out_specs=pl.BlockSpec((tm, d), lambda i: (i, 0)),
        out_shape=jax.ShapeDtypeStruct((t, d), BF16),
        compiler_params=_cp(("arbitrary",)),
    )(x2, scale, shift)


def _mm_kernel(a_ref, b_ref, o_ref):
    o_ref[...] = _dot(a_ref[...], b_ref[...]).astype(o_ref.dtype)


def _matmul(a, b, out_dtype):
    m, k = a.shape
    n = b.shape[1]
    tm = _pick(m, (1024, 512, 256, 128))
    tn = _pick(n, (1024, 768, 512, 384, 256, 128))
    return pl.pallas_call(
        _mm_kernel,
        grid=(m // tm, n // tn),
        in_specs=[pl.BlockSpec((tm, k), lambda i, j: (i, 0)),
                  pl.BlockSpec((k, tn), lambda i, j: (0, j))],
        out_specs=pl.BlockSpec((tm, tn), lambda i, j: (i, j)),
        out_shape=jax.ShapeDtypeStruct((m, n), out_dtype),
        compiler_params=_cp(("arbitrary", "arbitrary")),
    )(a, b)


def _mm3_kernel(a1_ref, a2_ref, a3_ref, b1_ref, b2_ref, b3_ref, o_ref):
    o_ref[...] = (_dot(a1_ref[...], b1_ref[...]) + _dot(a2_ref[...], b2_ref[...])
                  + _dot(a3_ref[...], b3_ref[...]))


def _out_proj(y_ssd, y_hg, y_ret, w_out):
    m = y_ssd.shape[0]
    d = w_out.shape[1]
    k1, k2, k3 = y_ssd.shape[1], y_hg.shape[1], y_ret.shape[1]
    tm = _pick(m, (1024, 512, 256, 128))
    tn = _pick(d, (1024, 512, 256, 128))
    return pl.pallas_call(
        _mm3_kernel,
        grid=(m // tm, d // tn),
        in_specs=[pl.BlockSpec((tm, k1), lambda i, j: (i, 0)),
                  pl.BlockSpec((tm, k2), lambda i, j: (i, 0)),
                  pl.BlockSpec((tm, k3), lambda i, j: (i, 0)),
                  pl.BlockSpec((k1, tn), lambda i, j: (0, j)),
                  pl.BlockSpec((k2, tn), lambda i, j: (k1 // k2, j)),
                  pl.BlockSpec((k3, tn), lambda i, j: ((k1 + k2) // k3, j))],
        out_specs=pl.BlockSpec((tm, tn), lambda i, j: (i, j)),
        out_shape=jax.ShapeDtypeStruct((m, d), F32),
        compiler_params=_cp(("arbitrary", "arbitrary")),
    )(y_ssd, y_hg, y_ret, w_out, w_out, w_out)


def _ssd_kernel(z_ref, xs_ref, b_ref, c_ref, dt_ref,
                wx_ref, wb_ref, wc_ref, bx_ref, bb_ref, bc_ref,
                dtb_ref, alog_ref, dskip_ref, nw_ref, e_ref, sel_ref,
                o_ref, st_ref, px_ref, pb_ref, pc_ref, *, heads_per_group):
    L = SSD_L
    ci = pl.program_id(2)

    @pl.when(ci == 0)
    def _():
        st_ref[...] = jnp.zeros_like(st_ref)
        px_ref[0:8, :] = jnp.zeros((8, px_ref.shape[1]), F32)
        pb_ref[0:8, :] = jnp.zeros((8, pb_ref.shape[1]), F32)
        pc_ref[0:8, :] = jnp.zeros((8, pc_ref.shape[1]), F32)

    def conv_silu(raw_ref, pad_ref, w_ref, bias_ref):
        pad_ref[8:8 + L, :] = raw_ref[...]
        acc = bias_ref[...]
        for k in range(SSD_CONV):
            acc = acc + w_ref[k:k + 1, :] * pad_ref[pl.ds(8 - (SSD_CONV - 1) + k, L), :]
        pad_ref[0:8, :] = pad_ref[L:L + 8, :]
        return _silu(acc)

    xs = conv_silu(xs_ref, px_ref, wx_ref, bx_ref)
    bm = conv_silu(b_ref, pb_ref, wb_ref, bb_ref)
    cm = conv_silu(c_ref, pc_ref, wc_ref, bc_ref)

    dt = jax.nn.softplus(dt_ref[...] + dtb_ref[...])
    dta = dt * (-jnp.exp(alog_ref[...]))
    tri = _tri(L)
    cum = _sel_left(tri.astype(BF16), dta)
    cum_g = _sel_right(cum, sel_ref[...])
    cum_gt = cum_g.T
    e01 = e_ref[...]
    dt_full = _sel_right(dt, e01)
    cum_full = _sel_right(cum, e01)
    last = cum_full[L - 1:L, :]
    to_end = jnp.exp(last - cum_full)
    from_start = jnp.exp(cum_full)
    xdt = xs * dt_full

    bm16 = bm.astype(BF16)
    cm16 = cm.astype(BF16)
    cb = _dot_nt(cm16, bm16)
    lane = lax.broadcasted_iota(I32, (L, LANES), 1)
    lo_half = lane < SSD_HEAD_DIM
    ys = []
    for j in range(heads_per_group // 2):
        atts = []
        for i in (2 * j, 2 * j + 1):
            dec = jnp.where(tri, jnp.exp(jnp.minimum(cum_g[:, i:i + 1] - cum_gt[i:i + 1, :], 0.0)), 0.0)
            atts.append((cb * dec).astype(BF16))
        xp = xdt[:, j * LANES:(j + 1) * LANES]
        stack = jnp.concatenate([jnp.where(lo_half, xp, 0.0), jnp.where(lo_half, 0.0, xp)], axis=0)
        ys.append(_dot(jnp.concatenate(atts, axis=1), stack.astype(BF16)))
    y = jnp.concatenate(ys, axis=1) if len(ys) > 1 else ys[0]

    st = st_ref[...]
    y = y + _dot(cm16, st.astype(BF16)) * from_start
    st_ref[...] = st * jnp.exp(last) + _dot(bm.T.astype(BF16), (xdt * to_end).astype(BF16))

    y = y + xs * dskip_ref[...]
    y = y * _silu(z_ref[...])
    y = y * lax.rsqrt(jnp.mean(jnp.square(y), axis=-1, keepdims=True) + NORM_EPS) * nw_ref[...]
    o_ref[...] = y.astype(o_ref.dtype)


def _ssd(proj, dt_raw, conv_w, conv_b, dt_bias, a_log, d_skip, norm_w, bsz, seq, ssd_w):
    t = proj.shape[0]
    L = SSD_L
    nc = seq // L
    g = SSD_GROUPS
    gw = ssd_w // g
    n = SSD_STATE
    heads = ssd_w // SSD_HEAD_DIM
    r = heads // g
    assert r % 2 == 0 and gw % LANES == 0 and seq % L == 0 and heads <= LANES
    off_b = 2 * ssd_w
    off_c = off_b + g * n
    hpad = LANES - heads
    dtb = jnp.pad(dt_bias, (0, hpad)).reshape(1, LANES)
    alog = jnp.pad(a_log, (0, hpad)).reshape(1, LANES)
    dskip = jnp.repeat(d_skip, SSD_HEAD_DIM).reshape(1, ssd_w)
    hid = jnp.arange(LANES)
    e01 = (hid[:, None] == (jnp.arange(ssd_w) // SSD_HEAD_DIM)[None, :]).astype(BF16)
    sel = (hid[None, :, None] == (jnp.arange(g)[:, None, None] * r + hid[None, None, :])
           ) & (hid[None, None, :] < r)
    sel = sel.astype(BF16)
    row = lambda b, gi, c: b * nc + c
    cw = conv_w
    cb2 = conv_b.reshape(1, -1)
    in_specs = [
        pl.BlockSpec((L, gw), lambda b, gi, c: (row(b, gi, c), gi)),
        pl.BlockSpec((L, gw), lambda b, gi, c: (row(b, gi, c), g + gi)),
        pl.BlockSpec((L, n), lambda b, gi, c: (row(b, gi, c), off_b // n + gi)),
        pl.BlockSpec((L, n), lambda b, gi, c: (row(b, gi, c), off_c // n + gi)),
        pl.BlockSpec((L, LANES), lambda b, gi, c: (row(b, gi, c), 0)),
        pl.BlockSpec((SSD_CONV, gw), lambda b, gi, c: (0, gi)),
        pl.BlockSpec((SSD_CONV, n), lambda b, gi, c: (0, ssd_w // n + gi)),
        pl.BlockSpec((SSD_CONV, n), lambda b, gi, c: (0, (ssd_w + g * n) // n + gi)),
        pl.BlockSpec((1, gw), lambda b, gi, c: (0, gi)),
        pl.BlockSpec((1, n), lambda b, gi, c: (0, ssd_w // n + gi)),
        pl.BlockSpec((1, n), lambda b, gi, c: (0, (ssd_w + g * n) // n + gi)),
        pl.BlockSpec((1, LANES), lambda b, gi, c: (0, 0)),
        pl.BlockSpec((1, LANES), lambda b, gi, c: (0, 0)),
        pl.BlockSpec((1, gw), lambda b, gi, c: (0, gi)),
        pl.BlockSpec((1, gw), lambda b, gi, c: (0, gi)),
        pl.BlockSpec((LANES, gw), lambda b, gi, c: (0, gi)),
        pl.BlockSpec((None, LANES, LANES), lambda b, gi, c: (gi, 0, 0)),
    ]
    return pl.pallas_call(
        functools.partial(_ssd_kernel, heads_per_group=r),
        grid=(bsz, g, nc),
        in_specs=in_specs,
        out_specs=pl.BlockSpec((L, gw), lambda b, gi, c: (row(b, gi, c), gi)),
        out_shape=jax.ShapeDtypeStruct((t, ssd_w), BF16),
        scratch_shapes=[pltpu.VMEM((n, gw), F32),
                        pltpu.VMEM((L + 8, gw), F32),
                        pltpu.VMEM((L + 8, n), F32),
                        pltpu.VMEM((L + 8, n), F32)],
        compiler_params=_cp(("arbitrary", "arbitrary", "arbitrary")),
    )(proj, proj, proj, proj, dt_raw, cw, cw, cw, cb2, cb2, cb2,
      dtb, alog, dskip, norm_w.reshape(1, ssd_w), e01, sel)


def _hgrn_kernel(q_ref, f_ref, v_ref, g_ref, lb_ref, nw_ref, o_ref, st_ref):
    L = HG_L
    SB = HG_SUB
    nsb = L // SB
    ci = pl.program_id(2)

    @pl.when(ci == 0)
    def _():
        st_ref[...] = jnp.zeros_like(st_ref)

    lb = lb_ref[...]
    nw = nw_ref[...]
    tri = _tri(L).astype(BF16)
    ones = jnp.ones((LANES, LANES), BF16)
    zpad = jnp.zeros((LANES - L, LANES), F32)
    lane = lax.broadcasted_iota(I32, (SB, LANES), 1)
    rowi = lax.broadcasted_iota(I32, (SB, LANES), 0)

    def chunk(c, carry):
        r0 = pl.multiple_of(c * L, L)
        q = q_ref[pl.ds(r0, L), :]
        fr = f_ref[pl.ds(r0, L), :]
        v = v_ref[pl.ds(r0, L), :]
        log_f = jnp.log(jnp.maximum(lb + (1.0 - lb) * jax.nn.sigmoid(fr), TINY))
        key = (1.0 - lb) * jax.nn.sigmoid(-fr)
        bcum = _sel_left(tri, log_f)
        last = bcum[L - 1:L, :]

        pieces = []
        for i in range(nsb):
            qi = q[i * SB:(i + 1) * SB]
            bi = bcum[i * SB:(i + 1) * SB]
            for j in range(SB):
                r = i * SB + j
                pieces.append((qi * key[r:r + 1] * jnp.exp(jnp.minimum(bi - bcum[r:r + 1], 0.0))).astype(BF16))
        sums = _dot(jnp.concatenate(pieces, axis=0), ones)

        att_rows = []
        for i in range(nsb):
            diag = jnp.zeros((SB, LANES), F32)
            for j in range(SB):
                r = i * SB + j
                diag = jnp.where(lane == r, sums[r * SB:(r + 1) * SB], diag)
            att_i = jnp.where(rowi + i * SB >= lane, diag, 0.0)
            if i > 0:
                edge = bcum[i * SB - 1:i * SB]
                qs = (q[i * SB:(i + 1) * SB] * jnp.exp(bcum[i * SB:(i + 1) * SB] - edge)).astype(BF16)
                ks = jnp.concatenate([key * jnp.exp(jnp.minimum(edge - bcum, 0.0)), zpad], axis=0).astype(BF16)
                att_i = jnp.where(lane < i * SB, _dot_nt(qs, ks), att_i)
            att_rows.append(att_i)
        att = jnp.concatenate(att_rows, axis=0)

        vpad = jnp.concatenate([v, zpad], axis=0)
        st = st_ref[...]
        o = (_dot_nt((q * jnp.exp(bcum)).astype(BF16), st.astype(BF16))
             + _dot(att.astype(BF16), vpad.astype(BF16)))
        kk = jnp.concatenate([key * jnp.exp(last - bcum), zpad], axis=0)
        st_ref[...] = st * jnp.exp(last) + _dot(vpad.T.astype(BF16), kk.astype(BF16))

        o = o * lax.rsqrt(jnp.mean(jnp.square(o), axis=-1, keepdims=True) + NORM_EPS) * nw
        o_ref[pl.ds(r0, L), :] = (o * _silu(g_ref[pl.ds(r0, L), :])).astype(o_ref.dtype)
        return carry

    lax.fori_loop(0, q_ref.shape[0] // L, chunk, 0, unroll=True)


def _hgrn(proj, lower_bound, norm_w, bsz, seq, hg_w, off_q):
    t = proj.shape[0]
    rows = _pick(seq, (4 * HG_L, 2 * HG_L, HG_L))
    nc = seq // rows
    h = hg_w // HEAD_V
    cq = off_q // LANES
    row = lambda b, hi, c: b * nc + c
    blk = lambda k: pl.BlockSpec((rows, LANES), lambda b, hi, c: (row(b, hi, c), cq + k * h + hi))
    vec = pl.BlockSpec((1, LANES), lambda b, hi, c: (0, hi))
    return pl.pallas_call(
        _hgrn_kernel,
        grid=(bsz, h, nc),
        in_specs=[blk(0), blk(1), blk(2), blk(3), vec, vec],
        out_specs=pl.BlockSpec((rows, LANES), lambda b, hi, c: (row(b, hi, c), hi)),
        out_shape=jax.ShapeDtypeStruct((t, hg_w), BF16),
        scratch_shapes=[pltpu.VMEM((HEAD_V, LANES), F32)],
        compiler_params=_cp(("arbitrary", "arbitrary", "arbitrary")),
    )(proj, proj, proj, proj, lower_bound.reshape(1, hg_w), norm_w.reshape(1, hg_w))


def _rope_kernel(pos_ref, freq_ref, cos_ref, sin_ref):
    ang = pos_ref[...].astype(F32) * freq_ref[...]
    lane = lax.broadcasted_iota(I32, ang.shape, 1)
    first = (lane % RET_KDIM) < (RET_KDIM // 2)
    cos_ref[...] = jnp.cos(ang)
    s = jnp.sin(ang)
    sin_ref[...] = jnp.where(first, -s, s)


def _rope_tables(positions):
    t = positions.size
    half = RET_KDIM // 2
    freq = ROPE_BASE ** (-jnp.arange(half, dtype=F32) / half)
    freq = jnp.tile(freq, LANES // half).reshape(1, LANES)
    tm = _pick(t, (512, 256, 128))
    return pl.pallas_call(
        _rope_kernel,
        grid=(t // tm,),
        in_specs=[pl.BlockSpec((tm, 1), lambda i: (i, 0)),
                  pl.BlockSpec((1, LANES), lambda i: (0, 0))],
        out_specs=[pl.BlockSpec((tm, LANES), lambda i: (i, 0))] * 2,
        out_shape=[jax.ShapeDtypeStruct((t, LANES), F32)] * 2,
        compiler_params=_cp(("arbitrary",)),
    )(positions.reshape(t, 1), freq)


def _ret_kernel(q_ref, k_ref, v_ref, g_ref, cos_ref, sin_ref, dm_ref, te_ref, fs_ref, cd_ref,
                nw_ref, nb_ref, o_ref, st_ref):
    L = RET_L
    ci = pl.program_id(2)

    @pl.when(ci == 0)
    def _():
        st_ref[...] = jnp.zeros_like(st_ref)

    cos = cos_ref[...]
    sin = sin_ref[...]
    lane = lax.broadcasted_iota(I32, (L, LANES), 1)
    rowi = lax.broadcasted_iota(I32, (L, LANES), 0)
    first = (lane % RET_KDIM) < (RET_KDIM // 2)

    def rope(x):
        sw = jnp.where(first, pltpu.roll(x, LANES - RET_KDIM // 2, 1), pltpu.roll(x, RET_KDIM // 2, 1))
        return x * cos + sw * sin

    qr = rope(q_ref[...])
    kr = rope(k_ref[...]) * RET_KDIM ** -0.5
    kr16 = kr.astype(BF16)
    v = v_ref[...]
    v16 = v.astype(BF16)

    kt = (kr * te_ref[...]).T
    cd = cd_ref[...]

    def summary(frames):
        full = _dot(jnp.where(frames, kt, 0.0).astype(BF16), v16)
        return jnp.where(rowi < RET_KDIM, full[:, :HEAD_V], full[:, HEAD_V:])
    s0 = st_ref[...]
    s1 = s0 * cd + summary(lane < CHUNK)
    st_ref[...] = s1 * cd + summary(lane >= CHUNK)
    s0_16 = s0.astype(BF16)
    s1_16 = s1.astype(BF16)

    outs = []
    for hh in range(2):
        qh = jnp.where((lane // RET_KDIM) == hh, qr, 0.0).astype(BF16)
        scores = _dot_nt(qh, kr16) * dm_ref[hh]
        o = _dot(scores.astype(BF16), v16[:, hh * HEAD_V:(hh + 1) * HEAD_V])
        inter = jnp.where(rowi < CHUNK, _dot(qh, s0_16), _dot(qh, s1_16))
        o = o + inter * fs_ref[hh]
        mu = jnp.mean(o, axis=-1, keepdims=True)
        var = jnp.mean(jnp.square(o - mu), axis=-1, keepdims=True)
        sl = slice(hh * HEAD_V, (hh + 1) * HEAD_V)
        o = (o - mu) * lax.rsqrt(var + NORM_EPS) * nw_ref[:, sl] + nb_ref[:, sl]
        outs.append(o * _silu(g_ref[:, sl]))
    o_ref[...] = jnp.concatenate(outs, axis=1).astype(o_ref.dtype)


def _retention(proj, cos, sin, norm_w, norm_b, bsz, seq, ret_w, off_q):
    t = proj.shape[0]
    L = RET_L
    nc = seq // L
    h = ret_w // HEAD_V
    pairs = h // 2
    assert h % 2 == 0 and seq % L == 0
    kw = h * RET_KDIM
    off_k = off_q + kw
    off_v = off_k + kw
    off_g = off_v + ret_w
    assert off_v % (2 * HEAD_V) == 0 and off_g % (2 * HEAD_V) == 0
    log_gamma = jnp.log1p(-jnp.exp2(-5.0 - jnp.arange(h, dtype=F32)))
    pos = jnp.arange(CHUNK, dtype=F32)
    d_intra = jnp.exp(log_gamma[:, None, None] * jnp.abs(pos[:, None] - pos[None, :]))
    dm = jnp.zeros((h, L, L), F32)
    dm = dm.at[:, :CHUNK, :CHUNK].set(d_intra).at[:, CHUNK:, CHUNK:].set(d_intra)
    to_end = jnp.exp(log_gamma[:, None] * (CHUNK - 1 - pos))
    te = jnp.repeat(jnp.tile(to_end, (1, 2)).reshape(pairs, 2, L), RET_KDIM, axis=1)
    te = jnp.swapaxes(te, 1, 2)
    from_start = jnp.exp(log_gamma[:, None] * (pos + 1.0))
    fs = jnp.broadcast_to(jnp.tile(from_start, (1, 2))[:, :, None], (h, L, HEAD_V))
    cdec = jnp.exp(log_gamma * CHUNK)
    cd = jnp.broadcast_to(jnp.repeat(cdec.reshape(pairs, 2), RET_KDIM, axis=1)[:, :, None],
                          (pairs, 2 * RET_KDIM, HEAD_V))
    row = lambda b, p, c: b * nc + c
    tab = lambda n0: pl.BlockSpec((n0, L, LANES), lambda b, p, c: (p, 0, 0))
    return pl.pallas_call(
        _ret_kernel,
        grid=(bsz, pairs, nc),
        in_specs=[pl.BlockSpec((L, LANES), lambda b, p, c: (row(b, p, c), off_q // LANES + p)),
                  pl.BlockSpec((L, LANES), lambda b, p, c: (row(b, p, c), off_k // LANES + p)),
                  pl.BlockSpec((L, 2 * HEAD_V), lambda b, p, c: (row(b, p, c), off_v // (2 * HEAD_V) + p)),
                  pl.BlockSpec((L, 2 * HEAD_V), lambda b, p, c: (row(b, p, c), off_g // (2 * HEAD_V) + p)),
                  pl.BlockSpec((L, LANES), lambda b, p, c: (row(b, p, c), 0)),
                  pl.BlockSpec((L, LANES), lambda b, p, c: (row(b, p, c), 0)),
                  tab(2),
                  pl.BlockSpec((None, L, LANES), lambda b, p, c: (p, 0, 0)),
                  tab(2),
                  pl.BlockSpec((None, 2 * RET_KDIM, HEAD_V), lambda b, p, c: (p, 0, 0)),
                  pl.BlockSpec((1, 2 * HEAD_V), lambda b, p, c: (0, p)),
                  pl.BlockSpec((1, 2 * HEAD_V), lambda b, p, c: (0, p))],
        out_specs=pl.BlockSpec((L, 2 * HEAD_V), lambda b, p, c: (row(b, p, c), p)),
        out_shape=jax.ShapeDtypeStruct((t, ret_w), BF16),
        scratch_shapes=[pltpu.VMEM((2 * RET_KDIM, HEAD_V), F32)],
        compiler_params=_cp(("arbitrary", "arbitrary", "arbitrary")),
    )(proj, proj, proj, proj, cos, sin, dm, te, fs, cd,
      norm_w.reshape(1, ret_w), norm_b.reshape(1, ret_w))


def _layer_norm_rows(v, g, b):
    mu = jnp.mean(v, axis=-1, keepdims=True)
    var = jnp.mean(jnp.square(v - mu), axis=-1, keepdims=True)
    return (v - mu) * lax.rsqrt(var + NORM_EPS) * g + b


def _route_rows(p):
    rows = [p[e:e + 1, :] for e in range(N_EXPERTS)]
    scores = []
    for gi in range(N_EXPERT_GROUPS):
        a, b, c, d = rows[gi * EXPERTS_PER_GROUP:(gi + 1) * EXPERTS_PER_GROUP]
        hi1, lo1 = jnp.maximum(a, b), jnp.minimum(a, b)
        hi2, lo2 = jnp.maximum(c, d), jnp.minimum(c, d)
        scores.append(jnp.maximum(hi1, hi2) + jnp.maximum(jnp.minimum(hi1, hi2), jnp.maximum(lo1, lo2)))
    best = scores[0]
    sel = jnp.zeros_like(best, dtype=I32)
    for gi in range(1, N_EXPERT_GROUPS):
        better = scores[gi] > best
        best = jnp.where(better, scores[gi], best)
        sel = jnp.where(better, gi, sel)
    cand = []
    for e in range(EXPERTS_PER_GROUP):
        v = rows[e]
        for gi in range(1, N_EXPERT_GROUPS):
            v = jnp.where(sel == gi, rows[gi * EXPERTS_PER_GROUP + e], v)
        cand.append(v)

    def argbest(vals):
        bv = vals[0]
        bi = jnp.zeros_like(sel)
        for e in range(1, len(vals)):
            better = vals[e] > bv
            bv = jnp.where(better, vals[e], bv)
            bi = jnp.where(better, e, bi)
        return bv, bi
    w0, i0 = argbest(cand)
    w1, i1 = argbest([jnp.where(i0 == e, -1.0, cand[e]) for e in range(EXPERTS_PER_GROUP)])
    tot = w0 + w1
    base = sel * EXPERTS_PER_GROUP
    return base + i0, base + i1, w0 / tot, w1 / tot


def _ln_kernel(*refs, alpha, pair_rows, emit_h, h_dtype, with_router):
    it = iter(refs)
    x_ref, y_ref = next(it), next(it)
    if pair_rows:
        y1_ref, w_ref = next(it), next(it)
    gate_ref, g_ref, b_ref = next(it), next(it), next(it)
    if emit_h:
        sc_ref, sh_ref = next(it), next(it)
    if with_router:
        rwh_ref, rwl_ref, rb_ref = next(it), next(it), next(it)
    xo_ref = next(it)
    if emit_h:
        h_ref = next(it)
    if with_router:
        idx_ref, wt_ref, cnt_ref = next(it), next(it), next(it)
    tm = x_ref.shape[0]
    y = y_ref[...]
    if pair_rows:
        y = y * w_ref[:, 0:1] + y1_ref[...] * w_ref[:, 1:2]
    xn = _layer_norm_rows(alpha * x_ref[...] + (1.0 + gate_ref[...]) * y, g_ref[...], b_ref[...])
    xo_ref[...] = xn
    if emit_h:
        h = xn * (1.0 + sc_ref[...]) + sh_ref[...]
        h_ref[...] = h.astype(h_dtype)
    if with_router:
        h_hi = h.astype(BF16)
        h_lo = (h - h_hi.astype(F32)).astype(BF16)
        rwh = rwh_ref[...]
        logits = (_dot_nt(rwh, h_hi) + _dot_nt(rwh, h_lo) + _dot_nt(rwl_ref[...], h_hi)
                  + rb_ref[...])
        e = jnp.exp(logits - jnp.max(logits, axis=0, keepdims=True))
        p = e / jnp.sum(e, axis=0, keepdims=True)
        i0, i1, w0, w1 = _route_rows(p)

        @pl.when(pl.program_id(0) == 0)
        def _():
            cnt_ref[...] = jnp.zeros_like(cnt_ref)
        erow = lax.broadcasted_iota(I32, logits.shape, 0)
        oh0 = (erow == i0).astype(F32)
        oh1 = (erow == i1).astype(F32)
        both = oh0 + oh1
        tr = lax.broadcasted_iota(I32, (tm, tm), 0)
        tc = lax.broadcasted_iota(I32, (tm, tm), 1)
        before = cnt_ref[:, 0:1] + _dot(both.astype(BF16), (tr < tc).astype(BF16))
        r0 = jnp.sum(oh0 * before, axis=0, keepdims=True).astype(I32)
        r1 = jnp.sum(oh1 * before, axis=0, keepdims=True).astype(I32)
        cnt_ref[...] = cnt_ref[...] + jnp.sum(both, axis=1, keepdims=True)
        zi = jnp.zeros((4, tm), I32)
        idx_ref[...] = jnp.concatenate([i0, i1, r0, r1, zi], axis=0)
        wt_ref[...] = jnp.concatenate([w0, w1, jnp.zeros((6, tm), F32)], axis=0)


def _post_norm(x2, y, gate, ln_g, ln_b, alpha, seq, *, second_rows=None, pair_w=None, scale=None,
               shift=None, h_dtype=BF16, router=None):
    t, d = x2.shape
    tm = _pick(seq, (128, 64))
    emit_h = scale is not None
    with_router = router is not None
    pair_rows = second_rows is not None
    vec = pl.BlockSpec((None, 1, d), lambda i: (i * tm // seq, 0, 0))
    par = pl.BlockSpec((1, d), lambda i: (0, 0))
    rows = pl.BlockSpec((tm, d), lambda i: (i, 0))
    in_specs = [rows, rows]
    args = [x2, y]
    if pair_rows:
        assert second_rows % tm == 0
        in_specs += [pl.BlockSpec((tm, d), lambda i: (second_rows // tm + i, 0)),
                     pl.BlockSpec((tm, TOP_K), lambda i: (i, 0))]
        args += [y, pair_w]
    in_specs += [vec, par, par]
    args += [gate, ln_g.reshape(1, d), ln_b.reshape(1, d)]
    out_specs = [rows]
    out_shape = [jax.ShapeDtypeStruct((t, d), F32)]
    if emit_h:
        in_specs += [vec, vec]
        args += [scale, shift]
        out_specs.append(rows)
        out_shape.append(jax.ShapeDtypeStruct((t, d), h_dtype))
    if with_router:
        rwh, rwl, rb = router
        e = rwh.shape[0]
        in_specs += [pl.BlockSpec((e, d), lambda i: (0, 0)), pl.BlockSpec((e, d), lambda i: (0, 0)),
                     pl.BlockSpec((e, 1), lambda i: (0, 0))]
        args += [rwh, rwl, rb]
        out_specs += [pl.BlockSpec((8, tm), lambda i: (0, i))] * 2
        out_shape += [jax.ShapeDtypeStruct((8, t), I32), jax.ShapeDtypeStruct((8, t), F32)]
    scratch = [pltpu.VMEM((N_EXPERTS, LANES), F32)] if with_router else []
    return pl.pallas_call(
        functools.partial(_ln_kernel, alpha=alpha, pair_rows=pair_rows, emit_h=emit_h,
                          h_dtype=h_dtype, with_router=with_router),
        grid=(t // tm,),
        in_specs=in_specs, out_specs=out_specs, out_shape=out_shape, scratch_shapes=scratch,
        compiler_params=_cp(("arbitrary",)),
    )(*args)


def _moe_kernel(src_ref, dst_ref, exp_ref, h_hbm, wg_ref, wu_ref, wd_ref, y_hbm,
                xbuf, ybuf, gsem, ssem):
    i = pl.program_id(0)
    nb = pl.num_programs(0)
    R = MOE_BLOCK
    slot = i % 2

    def gather(block, s):
        def body(r, c):
            tok = src_ref[block * R + r]
            pltpu.make_async_copy(h_hbm.at[pl.ds(tok, 1), :], xbuf.at[s, pl.ds(r, 1), :], gsem.at[s]).start()
            return c
        lax.fori_loop(0, R, body, 0, unroll=8)

    def wait_gather(s):
        pltpu.make_async_copy(h_hbm.at[pl.ds(0, R), :], xbuf.at[s], gsem.at[s]).wait()

    def wait_scatter(s):
        pltpu.make_async_copy(ybuf.at[s], y_hbm.at[pl.ds(0, R), :], ssem.at[s]).wait()

    @pl.when(i == 0)
    def _():
        gather(0, 0)

    @pl.when(i + 1 < nb)
    def _():
        gather(i + 1, 1 - slot)

    wait_gather(slot)
    x = xbuf[slot].astype(BF16)
    hid = _silu(_dot(x, wg_ref[...])) * _dot(x, wu_ref[...])
    yb = _dot(hid.astype(BF16), wd_ref[...])

    @pl.when(i >= 2)
    def _():
        wait_scatter(slot)
    ybuf[slot] = yb

    def scatter(r, c):
        row = dst_ref[i * R + r]
        pltpu.make_async_copy(ybuf.at[slot, pl.ds(r, 1), :], y_hbm.at[pl.ds(row, 1), :], ssem.at[slot]).start()
        return c
    lax.fori_loop(0, R, scatter, 0, unroll=8)

    @pl.when(i == nb - 1)
    def _():
        wait_scatter(slot)
        wait_scatter(1 - slot)


def _moe_blocks(h2, slot_src, slot_dst, block_exp, wg, wu, wd, n_rows_out):
    t, d = h2.shape
    f = wg.shape[2]
    n_blocks = block_exp.shape[0]
    R = MOE_BLOCK
    grid_spec = pltpu.PrefetchScalarGridSpec(
        num_scalar_prefetch=3,
        grid=(n_blocks,),
        in_specs=[pl.BlockSpec(memory_space=pl.ANY),
                  pl.BlockSpec((None, d, f), lambda i, s, dd, e: (e[i], 0, 0)),
                  pl.BlockSpec((None, d, f), lambda i, s, dd, e: (e[i], 0, 0)),
                  pl.BlockSpec((None, f, d), lambda i, s, dd, e: (e[i], 0, 0))],
        out_specs=pl.BlockSpec(memory_space=pl.ANY),
        scratch_shapes=[pltpu.VMEM((2, R, d), F32),
                        pltpu.VMEM((2, R, d), F32),
                        pltpu.SemaphoreType.DMA((2,)),
                        pltpu.SemaphoreType.DMA((2,))],
    )
    return pl.pallas_call(
        _moe_kernel,
        grid_spec=grid_spec,
        out_shape=jax.ShapeDtypeStruct((n_rows_out, d), F32),
        compiler_params=_cp(("arbitrary",)),
    )(slot_src, slot_dst, block_exp, h2, wg, wu, wd)


def _dispatch_plan(idx, t):
    n_assign = t * TOP_K
    e01 = idx[:TOP_K]
    rank01 = idx[TOP_K:2 * TOP_K]
    experts = jnp.arange(N_EXPERTS, dtype=I32)
    counts = jnp.sum((e01[:, :, None] == experts).astype(I32), axis=(0, 1))
    padded = (counts + MOE_BLOCK - 1) // MOE_BLOCK * MOE_BLOCK
    pad_end = jnp.cumsum(padded)
    pad_start = pad_end - padded
    real_end = jnp.cumsum(counts)
    dest = pad_start[e01] + rank01
    n_blocks = -(-n_assign // MOE_BLOCK) + N_EXPERTS
    n_slots = n_blocks * MOE_BLOCK
    block_start = jnp.arange(n_blocks, dtype=I32) * MOE_BLOCK
    block_exp = jnp.minimum(jnp.sum(block_start[:, None] >= pad_end[None, :], -1), N_EXPERTS - 1).astype(I32)
    second = t + n_slots - n_assign
    tok = jnp.arange(t, dtype=I32)
    out_row = jnp.stack([tok, second + tok])
    slot_row = jnp.full((n_slots,), -1, I32).at[dest.reshape(-1)].set(out_row.reshape(-1))
    s = jnp.arange(n_slots, dtype=I32)
    pad_rank = s - real_end[jnp.repeat(block_exp, MOE_BLOCK)]
    is_real = slot_row >= 0
    slot_dst = jnp.where(is_real, slot_row, t + pad_rank)
    slot_src = jnp.where(is_real, jnp.where(slot_row >= second, slot_row - second, slot_row), 0)
    return slot_src, slot_dst, block_exp, second


def kernel(x, c, positions, w_in, conv_w, conv_b, dt_bias, a_log, d_skip, ssd_norm_w, hgrn_gamma,
           hgrn_norm_w, ret_norm_w, ret_norm_b, w_out, ada_down, ada_up, ada_b, ln_g, ln_b,
           router_w, router_b, w_gate, w_up, w_down):
    bsz, seq, d = x.shape
    depth = w_in.shape[0]
    t = bsz * seq
    ssd_w = d // 2
    ssd_heads = ssd_w // SSD_HEAD_DIM
    conv_dim = ssd_w + 2 * SSD_GROUPS * SSD_STATE
    hg_w = d // 4
    ret_w = d // 4
    ret_kw = (ret_w // HEAD_V) * RET_KDIM
    alpha = (2.0 * depth) ** 0.25
    off_dt = ssd_w + conv_dim
    off_hq = off_dt
    off_rq = off_hq + 4 * hg_w

    p = jax.nn.softmax(hgrn_gamma.astype(F32), axis=0)
    lower_bounds = jnp.cumsum(p, axis=0) - p[0]
    mod = _modulation(c, ada_down, ada_up, ada_b)
    cos, sin = _rope_tables(positions)

    rwt = router_w.T
    rw_hi = rwt.astype(BF16)
    rw_lo = (rwt - rw_hi.astype(F32)).astype(BF16)
    rb = router_b.reshape(N_EXPERTS, 1).astype(F32)

    x2 = x.reshape(t, d)
    h = _modulate(x2, mod[0, 1], mod[0, 0], seq)
    for l in range(depth):
        shift1, scale1, gate1, shift2, scale2, gate2 = [mod[l, i] for i in range(N_MOD)]
        w_main = jnp.concatenate([w_in[l][:, :off_dt], w_in[l][:, off_dt + ssd_heads:]], axis=1).astype(BF16)
        w_dt = jnp.pad(w_in[l][:, off_dt:off_dt + ssd_heads], ((0, 0), (0, LANES - ssd_heads))).astype(BF16)
        proj = _matmul(h, w_main, F32)
        dt_raw = _matmul(h, w_dt, F32)
        y_ssd = _ssd(proj, dt_raw, conv_w[l], conv_b[l], dt_bias[l], a_log[l], d_skip[l], ssd_norm_w[l],
                     bsz, seq, ssd_w)
        y_hg = _hgrn(proj, lower_bounds[l], hgrn_norm_w[l], bsz, seq, hg_w, off_hq)
        y_ret = _retention(proj, cos, sin, ret_norm_w[l], ret_norm_b[l], bsz, seq, ret_w, off_rq)
        mixed = _out_proj(y_ssd, y_hg, y_ret, w_out[l].astype(BF16))
        x2, h2, idx, wts = _post_norm(x2, mixed, gate1, ln_g[l, 0], ln_b[l, 0], alpha, seq,
                                      scale=scale2, shift=shift2, h_dtype=F32, router=(rw_hi, rw_lo, rb))
        slot_src, slot_dst, block_exp, second = _dispatch_plan(idx, t)
        y2 = _moe_blocks(h2, slot_src, slot_dst, block_exp,
                         w_gate[l].astype(BF16), w_up[l].astype(BF16), w_down[l].astype(BF16), second + t)
        pair_w = wts[:TOP_K].T
        if l + 1 < depth:
            x2, h = _post_norm(x2, y2, gate2, ln_g[l, 1], ln_b[l, 1], alpha, seq, second_rows=second,
                               pair_w=pair_w, scale=mod[l + 1, 1], shift=mod[l + 1, 0])
        else:
            (x2,) = _post_norm(x2, y2, gate2, ln_g[l, 1], ln_b[l, 1], alpha, seq, second_rows=second,
                               pair_w=pair_w)
    return x2.reshape(bsz, seq, d)
```

```python
import functools
import math

import jax
import jax.numpy as jnp
import numpy as np
from jax import lax
from jax.experimental import pallas as pl
from jax.experimental.pallas import tpu as pltpu

F32 = jnp.float32
BF16 = jnp.bfloat16
I32 = jnp.int32

CHUNK = 64
TINY = 1e-30
NORM_EPS = 1e-5
ROPE_BASE = 10000.0
LANES = 128
SSD_HEAD_DIM = 64
SSD_GROUPS = 4
SSD_STATE = 128
SSD_CONV = 4
HEAD_V = 128
RET_KDIM = 64
N_EXPERTS = 16
N_EXPERT_GROUPS = 4
EXPERTS_PER_GROUP = N_EXPERTS // N_EXPERT_GROUPS
TOP_K = 2
MOE_BLOCK = 256
N_MOD = 6
VMEM_LIMIT = 56 * 1024 * 1024

SSD_L = 128
HG_L = 64
HG_SUB = 16
HG_SAFE_DECAY = 60.0
RET_L = 2 * CHUNK


def _cp(sem):
    return pltpu.CompilerParams(dimension_semantics=sem, vmem_limit_bytes=VMEM_LIMIT)


def _pick(n, cands):
    for c in cands:
        if n % c == 0:
            return c
    return n


def _silu(x):
    return x * jax.nn.sigmoid(x)


def _split3(v):
    hi = v.astype(BF16)
    r1 = v - hi.astype(F32)
    mid = r1.astype(BF16)
    lo = (r1 - mid.astype(F32)).astype(BF16)
    return hi, mid, lo


def _dot(a, b):
    return jnp.dot(a, b, preferred_element_type=F32)


def _dot_nt(a, b):
    return lax.dot_general(a, b, (((1,), (1,)), ((), ())), preferred_element_type=F32)


def _sel_right(v, m01):
    hi, mid, lo = _split3(v)
    return _dot(hi, m01) + _dot(mid, m01) + _dot(lo, m01)


def _sel_left(m01, v):
    hi, mid, lo = _split3(v)
    return _dot(m01, hi) + _dot(m01, mid) + _dot(m01, lo)


def _tri(n):
    r = lax.broadcasted_iota(I32, (n, n), 0)
    c = lax.broadcasted_iota(I32, (n, n), 1)
    return r >= c


def _mod_kernel(c_ref, down_ref, up_ref, b_ref, o_ref):
    c = c_ref[...]
    t = jnp.dot(_silu(c), down_ref[...], preferred_element_type=F32,
                precision=lax.Precision.HIGHEST)
    o_ref[...] = jnp.dot(t, up_ref[...], preferred_element_type=F32,
                         precision=lax.Precision.HIGHEST) + b_ref[...]


def _modulation(c, ada_down, ada_up, ada_b):
    depth, d, rank = ada_down.shape
    bsz = c.shape[0]
    rows = 8
    cp = jnp.zeros((rows, d), F32).at[:bsz].set(c)
    out = pl.pallas_call(
        _mod_kernel,
        grid=(depth, N_MOD),
        in_specs=[pl.BlockSpec((rows, d), lambda l, j: (0, 0)),
                  pl.BlockSpec((None, d, rank), lambda l, j: (l, 0, 0)),
                  pl.BlockSpec((None, rank, d), lambda l, j: (l, 0, j)),
                  pl.BlockSpec((None, 1, d), lambda l, j: (l, 0, j))],
        out_specs=pl.BlockSpec((None, None, rows, d), lambda l, j: (l, j, 0, 0)),
        out_shape=jax.ShapeDtypeStruct((depth, N_MOD, rows, d), F32),
        compiler_params=_cp(("arbitrary", "arbitrary")),
    )(cp, ada_down, ada_up, ada_b.reshape(depth, 1, N_MOD * d))
    return out[:, :, :bsz, None, :]


def _modulate_kernel(x_ref, sc_ref, sh_ref, h_ref):
    h_ref[...] = (x_ref[...] * (1.0 + sc_ref[...]) + sh_ref[...]).astype(h_ref.dtype)


def _modulate(x2, scale, shift, seq):
    t, d = x2.shape
    tm = _pick(seq, (512, 256, 128, 64))
    vec = pl.BlockSpec((None, 1, d), lambda i: (i * tm // seq, 0, 0))
    return pl.pallas_call(
        _modulate_kernel,
        grid=(t // tm,),
        in_specs=[pl.BlockSpec((tm, d), lambda i: (i, 0)), vec, vec],
        out_specs=pl.BlockSpec((tm, d), lambda i: (i, 0)),
        out_shape=jax.ShapeDtypeStruct((t, d), BF16),
        compiler_params=_cp(("arbitrary",)),
    )(x2, scale, shift)


def _mm_kernel(a_ref, b_ref, o_ref):
    o_ref[...] = _dot(a_ref[...], b_ref[...]).astype(o_ref.dtype)


def _matmul(a, b, out_dtype):
    m, k = a.shape
    n = b.shape[1]
    tm = _pick(m, (1024, 512, 256, 128))
    tn = _pick(n, (1024, 768, 512, 384, 256, 128))
    return pl.pallas_call(
        _mm_kernel,
        grid=(m // tm, n // tn),
        in_specs=[pl.BlockSpec((tm, k), lambda i, j: (i, 0)),
                  pl.BlockSpec((k, tn), lambda i, j: (0, j))],
        out_specs=pl.BlockSpec((tm, tn), lambda i, j: (i, j)),
        out_shape=jax.ShapeDtypeStruct((m, n), out_dtype),
        compiler_params=_cp(("arbitrary", "arbitrary")),
    )(a, b)


def _mm3_kernel(a1_ref, a2_ref, a3_ref, b1_ref, b2_ref, b3_ref, o_ref):
    o_ref[...] = (_dot(a1_ref[...], b1_ref[...]) + _dot(a2_ref[...], b2_ref[...])
                  + _dot(a3_ref[...], b3_ref[...]))


def _out_proj(y_ssd, y_hg, y_ret, w_out):
    m = y_ssd.shape[0]
    d = w_out.shape[1]
    k1, k2, k3 = y_ssd.shape[1], y_hg.shape[1], y_ret.shape[1]
    tm = _pick(m, (1024, 512, 256, 128))
    tn = _pick(d, (1024, 512, 256, 128))
    return pl.pallas_call(
        _mm3_kernel,
        grid=(m // tm, d // tn),
        in_specs=[pl.BlockSpec((tm, k1), lambda i, j: (i, 0)),
                  pl.BlockSpec((tm, k2), lambda i, j: (i, 0)),
                  pl.BlockSpec((tm, k3), lambda i, j: (i, 0)),
                  pl.BlockSpec((k1, tn), lambda i, j: (0, j)),
                  pl.BlockSpec((k2, tn), lambda i, j: (k1 // k2, j)),
                  pl.BlockSpec((k3, tn), lambda i, j: ((k1 + k2) // k3, j))],
        out_specs=pl.BlockSpec((tm, tn), lambda i, j: (i, j)),
        out_shape=jax.ShapeDtypeStruct((m, d), F32),
        compiler_params=_cp(("arbitrary", "arbitrary")),
    )(y_ssd, y_hg, y_ret, w_out, w_out, w_out)


def _ssd_kernel(z_ref, xs_ref, b_ref, c_ref, dt_ref,
                wx_ref, wb_ref, wc_ref, bx_ref, bb_ref, bc_ref,
                dtb_ref, alog_ref, dskip_ref, nw_ref, e_ref, sel_ref,
                o_ref, st_ref, px_ref, pb_ref, pc_ref, *, heads_per_group):
    L = SSD_L
    n = SSD_STATE
    groups = sel_ref.shape[0]
    gw = o_ref.shape[1] // groups
    ci = pl.program_id(1)

    @pl.when(ci == 0)
    def _():
        st_ref[...] = jnp.zeros_like(st_ref)
        px_ref[0:8, :] = jnp.zeros((8, px_ref.shape[1]), F32)
        pb_ref[0:8, :] = jnp.zeros((8, pb_ref.shape[1]), F32)
        pc_ref[0:8, :] = jnp.zeros((8, pc_ref.shape[1]), F32)

    def conv_silu(raw_ref, pad_ref, w_ref, bias_ref):
        pad_ref[8:8 + L, :] = raw_ref[...]
        acc = bias_ref[...]
        for k in range(SSD_CONV):
            acc = acc + w_ref[k:k + 1, :] * pad_ref[pl.ds(8 - (SSD_CONV - 1) + k, L), :]
        pad_ref[0:8, :] = pad_ref[L:L + 8, :]
        return _silu(acc)

    xs_all = conv_silu(xs_ref, px_ref, wx_ref, bx_ref)
    bm_all = conv_silu(b_ref, pb_ref, wb_ref, bb_ref)
    cm_all = conv_silu(c_ref, pc_ref, wc_ref, bc_ref)

    dt = jax.nn.softplus(dt_ref[...] + dtb_ref[...])
    dta = dt * (-jnp.exp(alog_ref[...]))
    tri = _tri(L)
    cum = _sel_left(tri.astype(BF16), dta)
    e01 = e_ref[...]
    dt_full_all = _sel_right(dt, e01)
    cum_full_all = _sel_right(cum, e01)
    lane = lax.broadcasted_iota(I32, (L, LANES), 1)
    lo_half = lane < SSD_HEAD_DIM

    for g in range(groups):
        sl = slice(g * gw, (g + 1) * gw)
        xs = xs_all[:, sl]
        bm = bm_all[:, g * n:(g + 1) * n]
        cm = cm_all[:, g * n:(g + 1) * n]
        cum_g = _sel_right(cum, sel_ref[g])
        cum_gt = cum_g.T
        cum_full = cum_full_all[:, sl]
        last = cum_full[L - 1:L, :]
        to_end = jnp.exp(last - cum_full)
        from_start = jnp.exp(cum_full)
        xdt = xs * dt_full_all[:, sl]

        bm16 = bm.astype(BF16)
        cm16 = cm.astype(BF16)
        cb = _dot_nt(cm16, bm16)
        ys = []
        for j in range(heads_per_group // 2):
            atts = []
            for i in (2 * j, 2 * j + 1):
                dec = jnp.where(tri, jnp.exp(jnp.minimum(cum_g[:, i:i + 1] - cum_gt[i:i + 1, :], 0.0)), 0.0)
                atts.append((cb * dec).astype(BF16))
            xp = xdt[:, j * LANES:(j + 1) * LANES]
            stack = jnp.concatenate([jnp.where(lo_half, xp, 0.0), jnp.where(lo_half, 0.0, xp)], axis=0)
            ys.append(_dot(jnp.concatenate(atts, axis=1), stack.astype(BF16)))
        y = jnp.concatenate(ys, axis=1) if len(ys) > 1 else ys[0]

        st = st_ref[g]
        y = y + _dot(cm16, st.astype(BF16)) * from_start
        st_ref[g] = st * jnp.exp(last) + _dot(bm.T.astype(BF16), (xdt * to_end).astype(BF16))

        y = y + xs * dskip_ref[:, sl]
        y = y * _silu(z_ref[:, sl])
        y = y * lax.rsqrt(jnp.mean(jnp.square(y), axis=-1, keepdims=True) + NORM_EPS) * nw_ref[:, sl]
        o_ref[:, sl] = y.astype(o_ref.dtype)


def _ssd(proj, dt_raw, conv_w, conv_b, dt_bias, a_log, d_skip, norm_w, bsz, seq, ssd_w):
    t = proj.shape[0]
    L = SSD_L
    nc = seq // L
    g = SSD_GROUPS
    gw = ssd_w // g
    n = SSD_STATE
    gn = g * n
    heads = ssd_w // SSD_HEAD_DIM
    r = heads // g
    assert r % 2 == 0 and gw % LANES == 0 and seq % L == 0 and heads <= LANES and ssd_w % gn == 0
    off_b = 2 * ssd_w
    off_c = off_b + gn
    hpad = LANES - heads
    dtb = jnp.pad(dt_bias, (0, hpad)).reshape(1, LANES)
    alog = jnp.pad(a_log, (0, hpad)).reshape(1, LANES)
    dskip = jnp.repeat(d_skip, SSD_HEAD_DIM).reshape(1, ssd_w)
    hid = jnp.arange(LANES)
    e01 = (hid[:, None] == (jnp.arange(ssd_w) // SSD_HEAD_DIM)[None, :]).astype(BF16)
    sel = (hid[None, :, None] == (jnp.arange(g)[:, None, None] * r + hid[None, None, :])
           ) & (hid[None, None, :] < r)
    sel = sel.astype(BF16)
    row = lambda b, c: b * nc + c
    cw = conv_w
    cb2 = conv_b.reshape(1, -1)
    fixed = lambda shape, col: pl.BlockSpec(shape, lambda b, c: (0, col))
    in_specs = [
        pl.BlockSpec((L, ssd_w), lambda b, c: (row(b, c), 0)),
        pl.BlockSpec((L, ssd_w), lambda b, c: (row(b, c), 1)),
        pl.BlockSpec((L, gn), lambda b, c: (row(b, c), off_b // gn)),
        pl.BlockSpec((L, gn), lambda b, c: (row(b, c), off_c // gn)),
        pl.BlockSpec((L, LANES), lambda b, c: (row(b, c), 0)),
        fixed((SSD_CONV, ssd_w), 0),
        fixed((SSD_CONV, gn), ssd_w // gn),
        fixed((SSD_CONV, gn), ssd_w // gn + 1),
        fixed((1, ssd_w), 0),
        fixed((1, gn), ssd_w // gn),
        fixed((1, gn), ssd_w // gn + 1),
        fixed((1, LANES), 0),
        fixed((1, LANES), 0),
        fixed((1, ssd_w), 0),
        fixed((1, ssd_w), 0),
        fixed((LANES, ssd_w), 0),
        pl.BlockSpec((g, LANES, LANES), lambda b, c: (0, 0, 0)),
    ]
    return pl.pallas_call(
        functools.partial(_ssd_kernel, heads_per_group=r),
        grid=(bsz, nc),
        in_specs=in_specs,
        out_specs=pl.BlockSpec((L, ssd_w), lambda b, c: (row(b, c), 0)),
        out_shape=jax.ShapeDtypeStruct((t, ssd_w), BF16),
        scratch_shapes=[pltpu.VMEM((g, n, gw), F32),
                        pltpu.VMEM((L + 8, ssd_w), F32),
                        pltpu.VMEM((L + 8, gn), F32),
                        pltpu.VMEM((L + 8, gn), F32)],
        compiler_params=_cp(("arbitrary", "arbitrary")),
    )(proj, proj, proj, proj, dt_raw, cw, cw, cw, cb2, cb2, cb2,
      dtb, alog, dskip, norm_w.reshape(1, ssd_w), e01, sel)


def _hgrn_kernel(q_ref, f_ref, v_ref, g_ref, lb_ref, nw_ref, o_ref, st_ref):
    L = HG_L
    SB = HG_SUB
    nsb = L // SB
    rows = q_ref.shape[0]
    ci = pl.program_id(2)

    @pl.when(ci == 0)
    def _():
        st_ref[...] = jnp.zeros_like(st_ref)

    lb = lb_ref[...]
    nw = nw_ref[...]
    tri = _tri(L).astype(BF16)
    ones = jnp.ones((LANES, LANES), BF16)
    zpad = jnp.zeros((LANES - L, LANES), F32)
    lane = lax.broadcasted_iota(I32, (SB, LANES), 1)
    rowi = lax.broadcasted_iota(I32, (SB, LANES), 0)

    def log_decay(fr):
        return jnp.log(jnp.maximum(lb + (1.0 - lb) * jax.nn.sigmoid(fr), TINY))

    sub_decay = jnp.sum((-log_decay(f_ref[...])).reshape(rows // SB, SB, LANES), axis=1)
    small_decay = jnp.max(sub_decay) < HG_SAFE_DECAY

    def scores_factored(q, key, bcum):
        att_rows = []
        for i in range(nsb):
            lo, hi = i * SB, (i + 1) * SB
            edge = bcum[lo - 1:lo] if i > 0 else jnp.zeros((1, LANES), F32)
            qs = (q[lo:hi] * jnp.exp(bcum[lo:hi] - edge)).astype(BF16)
            ks = key[:hi] * jnp.exp(edge - bcum[:hi])
            ks = jnp.concatenate([ks, jnp.zeros((LANES - hi, LANES), F32)], axis=0).astype(BF16)
            att_rows.append(jnp.where(rowi + lo >= lane, _dot_nt(qs, ks), 0.0))
        return jnp.concatenate(att_rows, axis=0)

    def scores_exact(q, key, bcum):
        pieces = []
        for i in range(nsb):
            qi = q[i * SB:(i + 1) * SB]
            bi = bcum[i * SB:(i + 1) * SB]
            for j in range(SB):
                r = i * SB + j
                pieces.append((qi * key[r:r + 1] * jnp.exp(jnp.minimum(bi - bcum[r:r + 1], 0.0))).astype(BF16))
        sums = _dot(jnp.concatenate(pieces, axis=0), ones)

        att_rows = []
        for i in range(nsb):
            diag = jnp.zeros((SB, LANES), F32)
            for j in range(SB):
                r = i * SB + j
                diag = jnp.where(lane == r, sums[r * SB:(r + 1) * SB], diag)
            att_i = jnp.where(rowi + i * SB >= lane, diag, 0.0)
            if i > 0:
                edge = bcum[i * SB - 1:i * SB]
                qs = (q[i * SB:(i + 1) * SB] * jnp.exp(bcum[i * SB:(i + 1) * SB] - edge)).astype(BF16)
                ks = jnp.concatenate([key * jnp.exp(jnp.minimum(edge - bcum, 0.0)), zpad], axis=0).astype(BF16)
                att_i = jnp.where(lane < i * SB, _dot_nt(qs, ks), att_i)
            att_rows.append(att_i)
        return jnp.concatenate(att_rows, axis=0)

    def chunk(c, scores):
        r0 = c * L
        q = q_ref[r0:r0 + L, :]
        fr = f_ref[r0:r0 + L, :]
        v = v_ref[r0:r0 + L, :]
        log_f = log_decay(fr)
        key = (1.0 - lb) * jax.nn.sigmoid(-fr)
        bcum = _sel_left(tri, log_f)
        last = bcum[L - 1:L, :]
        att = scores(q, key, bcum)

        vpad = jnp.concatenate([v, zpad], axis=0)
        st = st_ref[...]
        o = (_dot_nt((q * jnp.exp(bcum)).astype(BF16), st.astype(BF16))
             + _dot(att.astype(BF16), vpad.astype(BF16)))
        kk = jnp.concatenate([key * jnp.exp(last - bcum), zpad], axis=0)
        st_ref[...] = st * jnp.exp(last) + _dot(vpad.T.astype(BF16), kk.astype(BF16))

        o = o * lax.rsqrt(jnp.mean(jnp.square(o), axis=-1, keepdims=True) + NORM_EPS) * nw
        o_ref[r0:r0 + L, :] = (o * _silu(g_ref[r0:r0 + L, :])).astype(o_ref.dtype)

    @pl.when(small_decay)
    def _():
        for c in range(rows // L):
            chunk(c, scores_factored)

    @pl.when(jnp.logical_not(small_decay))
    def _():
        for c in range(rows // L):
            chunk(c, scores_exact)


def _hgrn(proj, lower_bound, norm_w, bsz, seq, hg_w, off_q):
    t = proj.shape[0]
    rows = _pick(seq, (8 * HG_L, 4 * HG_L, 2 * HG_L, HG_L))
    nc = seq // rows
    h = hg_w // HEAD_V
    cq = off_q // LANES
    row = lambda b, hi, c: b * nc + c
    blk = lambda k: pl.BlockSpec((rows, LANES), lambda b, hi, c: (row(b, hi, c), cq + k * h + hi))
    vec = pl.BlockSpec((1, LANES), lambda b, hi, c: (0, hi))
    return pl.pallas_call(
        _hgrn_kernel,
        grid=(bsz, h, nc),
        in_specs=[blk(0), blk(1), blk(2), blk(3), vec, vec],
        out_specs=pl.BlockSpec((rows, LANES), lambda b, hi, c: (row(b, hi, c), hi)),
        out_shape=jax.ShapeDtypeStruct((t, hg_w), BF16),
        scratch_shapes=[pltpu.VMEM((HEAD_V, LANES), F32)],
        compiler_params=_cp(("arbitrary", "arbitrary", "arbitrary")),
    )(proj, proj, proj, proj, lower_bound.reshape(1, hg_w), norm_w.reshape(1, hg_w))


def _rope_kernel(pos_ref, freq_ref, cos_ref, sin_ref):
    ang = pos_ref[...].astype(F32) * freq_ref[...]
    lane = lax.broadcasted_iota(I32, ang.shape, 1)
    first = (lane % RET_KDIM) < (RET_KDIM // 2)
    cos_ref[...] = jnp.cos(ang)
    s = jnp.sin(ang)
    sin_ref[...] = jnp.where(first, -s, s)


def _rope_tables(positions):
    t = positions.size
    half = RET_KDIM // 2
    freq = ROPE_BASE ** (-jnp.arange(half, dtype=F32) / half)
    freq = jnp.tile(freq, LANES // half).reshape(1, LANES)
    tm = _pick(t, (512, 256, 128))
    return pl.pallas_call(
        _rope_kernel,
        grid=(t // tm,),
        in_specs=[pl.BlockSpec((tm, 1), lambda i: (i, 0)),
                  pl.BlockSpec((1, LANES), lambda i: (0, 0))],
        out_specs=[pl.BlockSpec((tm, LANES), lambda i: (i, 0))] * 2,
        out_shape=[jax.ShapeDtypeStruct((t, LANES), F32)] * 2,
        compiler_params=_cp(("arbitrary",)),
    )(positions.reshape(t, 1), freq)


def _ret_kernel(q_ref, k_ref, v_ref, g_ref, cos_ref, sin_ref, dm_ref, te_ref, fs_ref, cd_ref,
                nw_ref, nb_ref, o_ref, st_ref):
    L = RET_L
    n_pairs = te_ref.shape[0]
    ci = pl.program_id(2)

    @pl.when(ci == 0)
    def _():
        st_ref[...] = jnp.zeros_like(st_ref)

    cos = cos_ref[...]
    sin = sin_ref[...]
    lane = lax.broadcasted_iota(I32, (L, LANES), 1)
    rowi = lax.broadcasted_iota(I32, (L, LANES), 0)
    first = (lane % RET_KDIM) < (RET_KDIM // 2)

    def rope(x):
        sw = jnp.where(first, pltpu.roll(x, LANES - RET_KDIM // 2, 1), pltpu.roll(x, RET_KDIM // 2, 1))
        return x * cos + sw * sin

    for p in range(n_pairs):
        qr = rope(q_ref[:, p * LANES:(p + 1) * LANES])
        kr = rope(k_ref[:, p * LANES:(p + 1) * LANES]) * RET_KDIM ** -0.5
        kr16 = kr.astype(BF16)
        v16 = v_ref[:, 2 * p * HEAD_V:2 * (p + 1) * HEAD_V].astype(BF16)

        kt = (kr * te_ref[p]).T
        cd = cd_ref[p]

        def summary(frames):
            full = _dot(jnp.where(frames, kt, 0.0).astype(BF16), v16)
            return jnp.where(rowi < RET_KDIM, full[:, :HEAD_V], full[:, HEAD_V:])
        s0 = st_ref[p]
        s1 = s0 * cd + summary(lane < CHUNK)
        st_ref[p] = s1 * cd + summary(lane >= CHUNK)
        s0_16 = s0.astype(BF16)
        s1_16 = s1.astype(BF16)

        for hh in range(2):
            head = 2 * p + hh
            qh = jnp.where((lane // RET_KDIM) == hh, qr, 0.0).astype(BF16)
            scores = _dot_nt(qh, kr16) * dm_ref[head]
            o = _dot(scores.astype(BF16), v16[:, hh * HEAD_V:(hh + 1) * HEAD_V])
            inter = jnp.where(rowi < CHUNK, _dot(qh, s0_16), _dot(qh, s1_16))
            o = o + inter * fs_ref[head]
            mu = jnp.mean(o, axis=-1, keepdims=True)
            var = jnp.mean(jnp.square(o - mu), axis=-1, keepdims=True)
            sl = slice(head * HEAD_V, (head + 1) * HEAD_V)
            o = (o - mu) * lax.rsqrt(var + NORM_EPS) * nw_ref[:, sl] + nb_ref[:, sl]
            o_ref[:, sl] = (o * _silu(g_ref[:, sl])).astype(o_ref.dtype)


def _retention(proj, cos, sin, norm_w, norm_b, bsz, seq, ret_w, off_q):
    t = proj.shape[0]
    L = RET_L
    nc = seq // L
    h = ret_w // HEAD_V
    pairs = h // 2
    assert h % 2 == 0 and seq % L == 0
    kw = h * RET_KDIM
    off_k = off_q + kw
    off_v = off_k + kw
    off_g = off_v + ret_w
    assert off_v % (2 * HEAD_V) == 0 and off_g % (2 * HEAD_V) == 0
    log_gamma = jnp.log1p(-jnp.exp2(-5.0 - jnp.arange(h, dtype=F32)))
    pos = jnp.arange(CHUNK, dtype=F32)
    d_intra = jnp.exp(log_gamma[:, None, None] * jnp.abs(pos[:, None] - pos[None, :]))
    dm = jnp.zeros((h, L, L), F32)
    dm = dm.at[:, :CHUNK, :CHUNK].set(d_intra).at[:, CHUNK:, CHUNK:].set(d_intra)
    to_end = jnp.exp(log_gamma[:, None] * (CHUNK - 1 - pos))
    te = jnp.repeat(jnp.tile(to_end, (1, 2)).reshape(pairs, 2, L), RET_KDIM, axis=1)
    te = jnp.swapaxes(te, 1, 2)
    from_start = jnp.exp(log_gamma[:, None] * (pos + 1.0))
    fs = jnp.broadcast_to(jnp.tile(from_start, (1, 2))[:, :, None], (h, L, HEAD_V))
    cdec = jnp.exp(log_gamma * CHUNK)
    cd = jnp.broadcast_to(jnp.repeat(cdec.reshape(pairs, 2), RET_KDIM, axis=1)[:, :, None],
                          (pairs, 2 * RET_KDIM, HEAD_V))
    n_p = _pick(pairs, (4, 2, 1))
    wq = n_p * LANES
    wv = n_p * 2 * HEAD_V
    assert off_q % wq == 0 and off_k % wq == 0 and off_v % wv == 0 and off_g % wv == 0
    row = lambda b, p, c: b * nc + c
    tab = lambda n0, r: pl.BlockSpec((n0, r, LANES), lambda b, p, c: (p, 0, 0))
    return pl.pallas_call(
        _ret_kernel,
        grid=(bsz, pairs // n_p, nc),
        in_specs=[pl.BlockSpec((L, wq), lambda b, p, c: (row(b, p, c), off_q // wq + p)),
                  pl.BlockSpec((L, wq), lambda b, p, c: (row(b, p, c), off_k // wq + p)),
                  pl.BlockSpec((L, wv), lambda b, p, c: (row(b, p, c), off_v // wv + p)),
                  pl.BlockSpec((L, wv), lambda b, p, c: (row(b, p, c), off_g // wv + p)),
                  pl.BlockSpec((L, LANES), lambda b, p, c: (row(b, p, c), 0)),
                  pl.BlockSpec((L, LANES), lambda b, p, c: (row(b, p, c), 0)),
                  tab(2 * n_p, L),
                  tab(n_p, L),
                  tab(2 * n_p, L),
                  tab(n_p, 2 * RET_KDIM),
                  pl.BlockSpec((1, wv), lambda b, p, c: (0, p)),
                  pl.BlockSpec((1, wv), lambda b, p, c: (0, p))],
        out_specs=pl.BlockSpec((L, wv), lambda b, p, c: (row(b, p, c), p)),
        out_shape=jax.ShapeDtypeStruct((t, ret_w), BF16),
        scratch_shapes=[pltpu.VMEM((n_p, 2 * RET_KDIM, HEAD_V), F32)],
        compiler_params=_cp(("arbitrary", "arbitrary", "arbitrary")),
    )(proj, proj, proj, proj, cos, sin, dm, te, fs, cd,
      norm_w.reshape(1, ret_w), norm_b.reshape(1, ret_w))


def _layer_norm_rows(v, g, b):
    mu = jnp.mean(v, axis=-1, keepdims=True)
    var = jnp.mean(jnp.square(v - mu), axis=-1, keepdims=True)
    return (v - mu) * lax.rsqrt(var + NORM_EPS) * g + b


def _route_rows(p):
    rows = [p[e:e + 1, :] for e in range(N_EXPERTS)]
    scores = []
    for gi in range(N_EXPERT_GROUPS):
        a, b, c, d = rows[gi * EXPERTS_PER_GROUP:(gi + 1) * EXPERTS_PER_GROUP]
        hi1, lo1 = jnp.maximum(a, b), jnp.minimum(a, b)
        hi2, lo2 = jnp.maximum(c, d), jnp.minimum(c, d)
        scores.append(jnp.maximum(hi1, hi2) + jnp.maximum(jnp.minimum(hi1, hi2), jnp.maximum(lo1, lo2)))
    best = scores[0]
    sel = jnp.zeros_like(best, dtype=I32)
    for gi in range(1, N_EXPERT_GROUPS):
        better = scores[gi] > best
        best = jnp.where(better, scores[gi], best)
        sel = jnp.where(better, gi, sel)
    cand = []
    for e in range(EXPERTS_PER_GROUP):
        v = rows[e]
        for gi in range(1, N_EXPERT_GROUPS):
            v = jnp.where(sel == gi, rows[gi * EXPERTS_PER_GROUP + e], v)
        cand.append(v)

    def argbest(vals):
        bv = vals[0]
        bi = jnp.zeros_like(sel)
        for e in range(1, len(vals)):
            better = vals[e] > bv
            bv = jnp.where(better, vals[e], bv)
            bi = jnp.where(better, e, bi)
        return bv, bi
    w0, i0 = argbest(cand)
    w1, i1 = argbest([jnp.where(i0 == e, -1.0, cand[e]) for e in range(EXPERTS_PER_GROUP)])
    tot = w0 + w1
    base = sel * EXPERTS_PER_GROUP
    return base + i0, base + i1, w0 / tot, w1 / tot


def _ln_kernel(*refs, alpha, pair_rows, emit_h, h_dtype, with_router):
    it = iter(refs)
    x_ref, y_ref = next(it), next(it)
    if pair_rows:
        y1_ref, w_ref = next(it), next(it)
    gate_ref, g_ref, b_ref = next(it), next(it), next(it)
    if emit_h:
        sc_ref, sh_ref = next(it), next(it)
    if with_router:
        rwh_ref, rwl_ref, rb_ref = next(it), next(it), next(it)
    xo_ref = next(it)
    if emit_h:
        h_ref = next(it)
    if with_router:
        idx_ref, wt_ref, cnt_ref = next(it), next(it), next(it)
    tm = x_ref.shape[0]
    y = y_ref[...]
    if pair_rows:
        y = y * w_ref[:, 0:1] + y1_ref[...] * w_ref[:, 1:2]
    xn = _layer_norm_rows(alpha * x_ref[...] + (1.0 + gate_ref[...]) * y, g_ref[...], b_ref[...])
    xo_ref[...] = xn
    if emit_h:
        h = xn * (1.0 + sc_ref[...]) + sh_ref[...]
        h_ref[...] = h.astype(h_dtype)
    if with_router:
        h_hi = h.astype(BF16)
        h_lo = (h - h_hi.astype(F32)).astype(BF16)
        rwh = rwh_ref[...]
        logits = (_dot_nt(rwh, h_hi) + _dot_nt(rwh, h_lo) + _dot_nt(rwl_ref[...], h_hi)
                  + rb_ref[...])
        e = jnp.exp(logits - jnp.max(logits, axis=0, keepdims=True))
        p = e / jnp.sum(e, axis=0, keepdims=True)
        i0, i1, w0, w1 = _route_rows(p)

        @pl.when(pl.program_id(0) == 0)
        def _():
            cnt_ref[...] = jnp.zeros_like(cnt_ref)
        erow = lax.broadcasted_iota(I32, logits.shape, 0)
        oh0 = (erow == i0).astype(F32)
        oh1 = (erow == i1).astype(F32)
        both = oh0 + oh1
        tr = lax.broadcasted_iota(I32, (tm, tm), 0)
        tc = lax.broadcasted_iota(I32, (tm, tm), 1)
        before = cnt_ref[:, 0:1] + _dot(both.astype(BF16), (tr < tc).astype(BF16))
        r0 = jnp.sum(oh0 * before, axis=0, keepdims=True).astype(I32)
        r1 = jnp.sum(oh1 * before, axis=0, keepdims=True).astype(I32)
        cnt_ref[...] = cnt_ref[...] + jnp.sum(both, axis=1, keepdims=True)
        zi = jnp.zeros((4, tm), I32)
        idx_ref[...] = jnp.concatenate([i0, i1, r0, r1, zi], axis=0)
        wt_ref[...] = jnp.concatenate([w0, w1, jnp.zeros((6, tm), F32)], axis=0)


def _post_norm(x2, y, gate, ln_g, ln_b, alpha, seq, *, second_rows=None, pair_w=None, scale=None,
               shift=None, h_dtype=BF16, router=None):
    t, d = x2.shape
    tm = _pick(seq, (128, 64))
    emit_h = scale is not None
    with_router = router is not None
    pair_rows = second_rows is not None
    vec = pl.BlockSpec((None, 1, d), lambda i: (i * tm // seq, 0, 0))
    par = pl.BlockSpec((1, d), lambda i: (0, 0))
    rows = pl.BlockSpec((tm, d), lambda i: (i, 0))
    in_specs = [rows, rows]
    args = [x2, y]
    if pair_rows:
        assert second_rows % tm == 0
        in_specs += [pl.BlockSpec((tm, d), lambda i: (second_rows // tm + i, 0)),
                     pl.BlockSpec((tm, TOP_K), lambda i: (i, 0))]
        args += [y, pair_w]
    in_specs += [vec, par, par]
    args += [gate, ln_g.reshape(1, d), ln_b.reshape(1, d)]
    out_specs = [rows]
    out_shape = [jax.ShapeDtypeStruct((t, d), F32)]
    if emit_h:
        in_specs += [vec, vec]
        args += [scale, shift]
        out_specs.append(rows)
        out_shape.append(jax.ShapeDtypeStruct((t, d), h_dtype))
    if with_router:
        rwh, rwl, rb = router
        e = rwh.shape[0]
        in_specs += [pl.BlockSpec((e, d), lambda i: (0, 0)), pl.BlockSpec((e, d), lambda i: (0, 0)),
                     pl.BlockSpec((e, 1), lambda i: (0, 0))]
        args += [rwh, rwl, rb]
        out_specs += [pl.BlockSpec((8, tm), lambda i: (0, i))] * 2
        out_shape += [jax.ShapeDtypeStruct((8, t), I32), jax.ShapeDtypeStruct((8, t), F32)]
    scratch = [pltpu.VMEM((N_EXPERTS, LANES), F32)] if with_router else []
    return pl.pallas_call(
        functools.partial(_ln_kernel, alpha=alpha, pair_rows=pair_rows, emit_h=emit_h,
                          h_dtype=h_dtype, with_router=with_router),
        grid=(t // tm,),
        in_specs=in_specs, out_specs=out_specs, out_shape=out_shape, scratch_shapes=scratch,
        compiler_params=_cp(("arbitrary",)),
    )(*args)


def _moe_kernel(src_ref, dst_ref, exp_ref, h_hbm, wg_ref, wu_ref, wd_ref, y_hbm,
                xbuf0, xbuf1, ybuf0, ybuf1, gsem, ssem):
    i = pl.program_id(0)
    nb = pl.num_programs(0)
    R = MOE_BLOCK
    xbufs = (xbuf0, xbuf1)
    ybufs = (ybuf0, ybuf1)

    def gather_row(block, s, r):
        tok = src_ref[block * R + r]
        pltpu.make_async_copy(h_hbm.at[pl.ds(tok, 1), :], xbufs[s].at[pl.ds(r, 1), :], gsem.at[s]).start()

    def scatter_row(ext_block, s, r):
        row = dst_ref[ext_block * R + r]
        pltpu.make_async_copy(ybufs[s].at[pl.ds(r, 1), :], y_hbm.at[pl.ds(row, 1), :], ssem.at[s]).start()

    def wait_gather(s):
        pltpu.make_async_copy(h_hbm.at[pl.ds(0, R), :], xbufs[s], gsem.at[s]).wait()

    def wait_scatter(s):
        pltpu.make_async_copy(ybufs[s], y_hbm.at[pl.ds(0, R), :], ssem.at[s]).wait()

    @pl.when(i == 0)
    def _():
        ybuf1[...] = jnp.zeros(ybuf1.shape, F32)

        def first(r, c):
            gather_row(0, 0, r)
            return c
        lax.fori_loop(0, R, first, 0, unroll=8)

    def step(slot):
        other = 1 - slot
        wait_gather(slot)
        nxt = jnp.minimum(i + 1, nb - 1)
        x = xbufs[slot][...].astype(BF16)
        hid = _silu(_dot(x, wg_ref[...])) * _dot(x, wu_ref[...])
        yb = _dot(hid.astype(BF16), wd_ref[...])
        for r in range(R):
            gather_row(nxt, other, r)
            scatter_row(i, other, r)

        @pl.when(i >= 1)
        def _():
            wait_scatter(slot)
        ybufs[slot][...] = yb

        @pl.when(i == nb - 1)
        def _():
            def last(r, c):
                scatter_row(i + 1, slot, r)
                return c
            lax.fori_loop(0, R, last, 0, unroll=8)
            wait_gather(other)
            wait_scatter(other)
            wait_scatter(slot)

    for parity in range(2):
        pl.when(i % 2 == parity)(functools.partial(step, parity))


def _moe_blocks(h2, slot_src, slot_dst, block_exp, wg, wu, wd, n_rows_out):
    t, d = h2.shape
    f = wg.shape[2]
    n_blocks = block_exp.shape[0]
    R = MOE_BLOCK
    dst_ext = jnp.concatenate([n_rows_out + jnp.arange(R, dtype=I32), slot_dst])
    grid_spec = pltpu.PrefetchScalarGridSpec(
        num_scalar_prefetch=3,
        grid=(n_blocks,),
        in_specs=[pl.BlockSpec(memory_space=pl.ANY),
                  pl.BlockSpec((None, d, f), lambda i, s, dd, e: (e[i], 0, 0)),
                  pl.BlockSpec((None, d, f), lambda i, s, dd, e: (e[i], 0, 0)),
                  pl.BlockSpec((None, f, d), lambda i, s, dd, e: (e[i], 0, 0))],
        out_specs=pl.BlockSpec(memory_space=pl.ANY),
        scratch_shapes=[pltpu.VMEM((R, d), F32), pltpu.VMEM((R, d), F32),
                        pltpu.VMEM((R, d), F32), pltpu.VMEM((R, d), F32),
                        pltpu.SemaphoreType.DMA((2,)),
                        pltpu.SemaphoreType.DMA((2,))],
    )
    return pl.pallas_call(
        _moe_kernel,
        grid_spec=grid_spec,
        out_shape=jax.ShapeDtypeStruct((n_rows_out + R, d), F32),
        compiler_params=_cp(("arbitrary",)),
    )(slot_src, dst_ext, block_exp, h2, wg, wu, wd)


def _dispatch_plan(idx, t):
    n_assign = t * TOP_K
    e01 = idx[:TOP_K]
    rank01 = idx[TOP_K:2 * TOP_K]
    experts = jnp.arange(N_EXPERTS, dtype=I32)
    onehot = (e01[:, :, None] == experts).astype(I32)
    counts = jnp.sum(onehot, axis=(0, 1))
    padded = (counts + MOE_BLOCK - 1) // MOE_BLOCK * MOE_BLOCK
    pad_end = jnp.cumsum(padded)
    pad_start = pad_end - padded
    real_end = jnp.cumsum(counts)
    dest = jnp.sum(onehot * pad_start, axis=-1) + rank01
    n_blocks = -(-n_assign // MOE_BLOCK) + N_EXPERTS
    n_slots = n_blocks * MOE_BLOCK
    block_start = jnp.arange(n_blocks, dtype=I32) * MOE_BLOCK
    block_exp = jnp.minimum(jnp.sum(block_start[:, None] >= pad_end[None, :], -1), N_EXPERTS - 1).astype(I32)
    second = t + n_slots - n_assign
    tok = jnp.arange(t, dtype=I32)
    out_row = jnp.stack([tok, second + tok])
    slot_row = jnp.full((n_slots,), -1, I32).at[dest.reshape(-1)].set(out_row.reshape(-1))
    s = jnp.arange(n_slots, dtype=I32)
    pad_rank = s - jnp.repeat(real_end[block_exp], MOE_BLOCK)
    is_real = slot_row >= 0
    slot_dst = jnp.where(is_real, slot_row, t + pad_rank)
    slot_src = jnp.where(is_real, jnp.where(slot_row >= second, slot_row - second, slot_row), 0)
    return slot_src, slot_dst, block_exp, second


def kernel(x, c, positions, w_in, conv_w, conv_b, dt_bias, a_log, d_skip, ssd_norm_w, hgrn_gamma,
           hgrn_norm_w, ret_norm_w, ret_norm_b, w_out, ada_down, ada_up, ada_b, ln_g, ln_b,
           router_w, router_b, w_gate, w_up, w_down):
    bsz, seq, d = x.shape
    depth = w_in.shape[0]
    t = bsz * seq
    ssd_w = d // 2
    ssd_heads = ssd_w // SSD_HEAD_DIM
    conv_dim = ssd_w + 2 * SSD_GROUPS * SSD_STATE
    hg_w = d // 4
    ret_w = d // 4
    ret_kw = (ret_w // HEAD_V) * RET_KDIM
    alpha = (2.0 * depth) ** 0.25
    off_dt = ssd_w + conv_dim
    off_hq = off_dt
    off_rq = off_hq + 4 * hg_w

    p = jax.nn.softmax(hgrn_gamma.astype(F32), axis=0)
    lower_bounds = jnp.cumsum(p, axis=0) - p[0]
    mod = _modulation(c, ada_down, ada_up, ada_b)
    cos, sin = _rope_tables(positions)

    rwt = router_w.T
    rw_hi = rwt.astype(BF16)
    rw_lo = (rwt - rw_hi.astype(F32)).astype(BF16)
    rb = router_b.reshape(N_EXPERTS, 1).astype(F32)

    x2 = x.reshape(t, d)
    h = _modulate(x2, mod[0, 1], mod[0, 0], seq)
    for l in range(depth):
        shift1, scale1, gate1, shift2, scale2, gate2 = [mod[l, i] for i in range(N_MOD)]
        w_main = jnp.concatenate([w_in[l][:, :off_dt], w_in[l][:, off_dt + ssd_heads:]], axis=1).astype(BF16)
        w_dt = jnp.pad(w_in[l][:, off_dt:off_dt + ssd_heads], ((0, 0), (0, LANES - ssd_heads))).astype(BF16)
        proj = _matmul(h, w_main, F32)
        dt_raw = _matmul(h, w_dt, F32)
        y_ssd = _ssd(proj, dt_raw, conv_w[l], conv_b[l], dt_bias[l], a_log[l], d_skip[l], ssd_norm_w[l],
                     bsz, seq, ssd_w)
        y_hg = _hgrn(proj, lower_bounds[l], hgrn_norm_w[l], bsz, seq, hg_w, off_hq)
        y_ret = _retention(proj, cos, sin, ret_norm_w[l], ret_norm_b[l], bsz, seq, ret_w, off_rq)
        mixed = _out_proj(y_ssd, y_hg, y_ret, w_out[l].astype(BF16))
        x2, h2, idx, wts = _post_norm(x2, mixed, gate1, ln_g[l, 0], ln_b[l, 0], alpha, seq,
                                      scale=scale2, shift=shift2, h_dtype=F32, router=(rw_hi, rw_lo, rb))
        slot_src, slot_dst, block_exp, second = _dispatch_plan(idx, t)
        y2 = _moe_blocks(h2, slot_src, slot_dst, block_exp,
                         w_gate[l].astype(BF16), w_up[l].astype(BF16), w_down[l].astype(BF16), second + t)
        pair_w = wts[:TOP_K].T
        if l + 1 < depth:
            x2, h = _post_norm(x2, y2, gate2, ln_g[l, 1], ln_b[l, 1], alpha, seq, second_rows=second,
                               pair_w=pair_w, scale=mod[l + 1, 1], shift=mod[l + 1, 0])
        else:
            (x2,) = _post_norm(x2, y2, gate2, ln_g[l, 1], ln_b[l, 1], alpha, seq, second_rows=second,
                               pair_w=pair_w)
    return x2.reshape(bsz, seq, d)
```

```python
import functools
import math

import jax
import jax.numpy as jnp
import numpy as np
from jax import lax
from jax.experimental import pallas as pl
from jax.experimental.pallas import tpu as pltpu

F32 = jnp.float32
BF16 = jnp.bfloat16
I32 = jnp.int32

CHUNK = 64
TINY = 1e-30
NORM_EPS = 1e-5
ROPE_BASE = 10000.0
LANES = 128
SSD_HEAD_DIM = 64
SSD_GROUPS = 4
SSD_STATE = 128
SSD_CONV = 4
HEAD_V = 128
RET_KDIM = 64
N_EXPERTS = 16
N_EXPERT_GROUPS = 4
EXPERTS_PER_GROUP = N_EXPERTS // N_EXPERT_GROUPS
TOP_K = 2
MOE_BLOCK = 256
N_MOD = 6
VMEM_LIMIT = 56 * 1024 * 1024

SSD_L = 128
HG_L = 64
HG_SUB = 16
HG_SAFE_DECAY = 60.0
RET_L = 2 * CHUNK


def _cp(sem):
    return pltpu.CompilerParams(dimension_semantics=sem, vmem_limit_bytes=VMEM_LIMIT)


def _pick(n, cands):
    for c in cands:
        if n % c == 0:
            return c
    return n


def _silu(x):
    return x * jax.nn.sigmoid(x)


def _split3(v):
    hi = v.astype(BF16)
    r1 = v - hi.astype(F32)
    mid = r1.astype(BF16)
    lo = (r1 - mid.astype(F32)).astype(BF16)
    return hi, mid, lo


def _dot(a, b):
    return jnp.dot(a, b, preferred_element_type=F32)


def _dot_nt(a, b):
    return lax.dot_general(a, b, (((1,), (1,)), ((), ())), preferred_element_type=F32)


def _sel_right(v, m01):
    hi, mid, lo = _split3(v)
    return _dot(hi, m01) + _dot(mid, m01) + _dot(lo, m01)


def _sel_left(m01, v):
    hi, mid, lo = _split3(v)
    return _dot(m01, hi) + _dot(m01, mid) + _dot(m01, lo)


def _tri(n):
    r = lax.broadcasted_iota(I32, (n, n), 0)
    c = lax.broadcasted_iota(I32, (n, n), 1)
    return r >= c


def _mod_kernel(c_ref, down_ref, up_ref, b_ref, o_ref):
    c = c_ref[...]
    t = jnp.dot(_silu(c), down_ref[...], preferred_element_type=F32,
                precision=lax.Precision.HIGHEST)
    o_ref[...] = jnp.dot(t, up_ref[...], preferred_element_type=F32,
                         precision=lax.Precision.HIGHEST) + b_ref[...]


def _modulation(c, ada_down, ada_up, ada_b):
    depth, d, rank = ada_down.shape
    bsz = c.shape[0]
    rows = 8
    cp = jnp.zeros((rows, d), F32).at[:bsz].set(c)
    out = pl.pallas_call(
        _mod_kernel,
        grid=(depth, N_MOD),
        in_specs=[pl.BlockSpec((rows, d), lambda l, j: (0, 0)),
                  pl.BlockSpec((None, d, rank), lambda l, j: (l, 0, 0)),
                  pl.BlockSpec((None, rank, d), lambda l, j: (l, 0, j)),
                  pl.BlockSpec((None, 1, d), lambda l, j: (l, 0, j))],
        out_specs=pl.BlockSpec((None, None, rows, d), lambda l, j: (l, j, 0, 0)),
        out_shape=jax.ShapeDtypeStruct((depth, N_MOD, rows, d), F32),
        compiler_params=_cp(("arbitrary", "arbitrary")),
    )(cp, ada_down, ada_up, ada_b.reshape(depth, 1, N_MOD * d))
    return out[:, :, :bsz, None, :]


def _modulate_kernel(x_ref, sc_ref, sh_ref, h_ref):
    h_ref[...] = (x_ref[...] * (1.0 + sc_ref[...]) + sh_ref[...]).astype(h_ref.dtype)


def _modulate(x2, scale, shift, seq):
    t, d = x2.shape
    tm = _pick(seq, (512, 256, 128, 64))
    vec = pl.BlockSpec((None, 1, d), lambda i: (i * tm // seq, 0, 0))
    return pl.pallas_call(
        _modulate_kernel,
        grid=(t // tm,),
        in_specs=[pl.BlockSpec((tm, d), lambda i: (i, 0)), vec, vec],
        out_specs=pl.BlockSpec((tm, d), lambda i: (i, 0)),
        out_shape=jax.ShapeDtypeStruct((t, d), BF16),
        compiler_params=_cp(("arbitrary",)),
    )(x2, scale, shift)


def _mm_kernel(a_ref, b_ref, o_ref):
    o_ref[...] = _dot(a_ref[...], b_ref[...]).astype(o_ref.dtype)


def _matmul(a, b, out_dtype):
    m, k = a.shape
    n = b.shape[1]
    tm = _pick(m, (1024, 512, 256, 128))
    tn = _pick(n, (1024, 768, 512, 384, 256, 128))
    return pl.pallas_call(
        _mm_kernel,
        grid=(m // tm, n // tn),
        in_specs=[pl.BlockSpec((tm, k), lambda i, j: (i, 0)),
                  pl.BlockSpec((k, tn), lambda i, j: (0, j))],
        out_specs=pl.BlockSpec((tm, tn), lambda i, j: (i, j)),
        out_shape=jax.ShapeDtypeStruct((m, n), out_dtype),
        compiler_params=_cp(("arbitrary", "arbitrary")),
    )(a, b)


def _mm3_kernel(a1_ref, a2_ref, a3_ref, b1_ref, b2_ref, b3_ref, o_ref):
    o_ref[...] = (_dot(a1_ref[...], b1_ref[...]) + _dot(a2_ref[...], b2_ref[...])
                  + _dot(a3_ref[...], b3_ref[...]))


def _out_proj(y_ssd, y_hg, y_ret, w_out):
    m = y_ssd.shape[0]
    d = w_out.shape[1]
    k1, k2, k3 = y_ssd.shape[1], y_hg.shape[1], y_ret.shape[1]
    tm = _pick(m, (1024, 512, 256, 128))
    tn = _pick(d, (1024, 512, 256, 128))
    return pl.pallas_call(
        _mm3_kernel,
        grid=(m // tm, d // tn),
        in_specs=[pl.BlockSpec((tm, k1), lambda i, j: (i, 0)),
                  pl.BlockSpec((tm, k2), lambda i, j: (i, 0)),
                  pl.BlockSpec((tm, k3), lambda i, j: (i, 0)),
                  pl.BlockSpec((k1, tn), lambda i, j: (0, j)),
                  pl.BlockSpec((k2, tn), lambda i, j: (k1 // k2, j)),
                  pl.BlockSpec((k3, tn), lambda i, j: ((k1 + k2) // k3, j))],
        out_specs=pl.BlockSpec((tm, tn), lambda i, j: (i, j)),
        out_shape=jax.ShapeDtypeStruct((m, d), F32),
        compiler_params=_cp(("arbitrary", "arbitrary")),
    )(y_ssd, y_hg, y_ret, w_out, w_out, w_out)


def _ssd_kernel(z_ref, xs_ref, b_ref, c_ref, dt_ref,
                wx_ref, wb_ref, wc_ref, bx_ref, bb_ref, bc_ref,
                dtb_ref, alog_ref, dskip_ref, nw_ref, e_ref, sel_ref,
                o_ref, st_ref, px_ref, pb_ref, pc_ref, *, heads_per_group):
    L = SSD_L
    n = SSD_STATE
    groups = sel_ref.shape[0]
    gw = o_ref.shape[1] // groups
    ci = pl.program_id(1)

    @pl.when(ci == 0)
    def _():
        st_ref[...] = jnp.zeros_like(st_ref)
        px_ref[0:8, :] = jnp.zeros((8, px_ref.shape[1]), F32)
        pb_ref[0:8, :] = jnp.zeros((8, pb_ref.shape[1]), F32)
        pc_ref[0:8, :] = jnp.zeros((8, pc_ref.shape[1]), F32)

    def conv_silu(raw_ref, pad_ref, w_ref, bias_ref):
        pad_ref[8:8 + L, :] = raw_ref[...]
        acc = bias_ref[...]
        for k in range(SSD_CONV):
            acc = acc + w_ref[k:k + 1, :] * pad_ref[pl.ds(8 - (SSD_CONV - 1) + k, L), :]
        pad_ref[0:8, :] = pad_ref[L:L + 8, :]
        return _silu(acc)

    xs_all = conv_silu(xs_ref, px_ref, wx_ref, bx_ref)
    bm_all = conv_silu(b_ref, pb_ref, wb_ref, bb_ref)
    cm_all = conv_silu(c_ref, pc_ref, wc_ref, bc_ref)

    dt = jax.nn.softplus(dt_ref[...] + dtb_ref[...])
    dta = dt * (-jnp.exp(alog_ref[...]))
    tri = _tri(L)
    cum = _sel_left(tri.astype(BF16), dta)
    e01 = e_ref[...]
    dt_full_all = _sel_right(dt, e01)
    cum_full_all = _sel_right(cum, e01)
    lane = lax.broadcasted_iota(I32, (L, LANES), 1)
    lo_half = lane < SSD_HEAD_DIM

    for g in range(groups):
        sl = slice(g * gw, (g + 1) * gw)
        xs = xs_all[:, sl]
        bm = bm_all[:, g * n:(g + 1) * n]
        cm = cm_all[:, g * n:(g + 1) * n]
        cum_g = _sel_right(cum, sel_ref[g])
        cum_gt = cum_g.T
        cum_full = cum_full_all[:, sl]
        last = cum_full[L - 1:L, :]
        to_end = jnp.exp(last - cum_full)
        from_start = jnp.exp(cum_full)
        xdt = xs * dt_full_all[:, sl]

        bm16 = bm.astype(BF16)
        cm16 = cm.astype(BF16)
        cb = _dot_nt(cm16, bm16)
        ys = []
        for j in range(heads_per_group // 2):
            atts = []
            for i in (2 * j, 2 * j + 1):
                dec = jnp.where(tri, jnp.exp(jnp.minimum(cum_g[:, i:i + 1] - cum_gt[i:i + 1, :], 0.0)), 0.0)
                atts.append((cb * dec).astype(BF16))
            xp = xdt[:, j * LANES:(j + 1) * LANES]
            stack = jnp.concatenate([jnp.where(lo_half, xp, 0.0), jnp.where(lo_half, 0.0, xp)], axis=0)
            ys.append(_dot(jnp.concatenate(atts, axis=1), stack.astype(BF16)))
        y = jnp.concatenate(ys, axis=1) if len(ys) > 1 else ys[0]

        st = st_ref[g]
        y = y + _dot(cm16, st.astype(BF16)) * from_start
        st_ref[g] = st * jnp.exp(last) + _dot(bm.T.astype(BF16), (xdt * to_end).astype(BF16))

        y = y + xs * dskip_ref[:, sl]
        y = y * _silu(z_ref[:, sl])
        y = y * lax.rsqrt(jnp.mean(jnp.square(y), axis=-1, keepdims=True) + NORM_EPS) * nw_ref[:, sl]
        o_ref[:, sl] = y.astype(o_ref.dtype)


def _ssd(proj, dt_raw, conv_w, conv_b, dt_bias, a_log, d_skip, norm_w, bsz, seq, ssd_w):
    t = proj.shape[0]
    L = SSD_L
    nc = seq // L
    g = SSD_GROUPS
    gw = ssd_w // g
    n = SSD_STATE
    gn = g * n
    heads = ssd_w // SSD_HEAD_DIM
    r = heads // g
    assert r % 2 == 0 and gw % LANES == 0 and seq % L == 0 and heads <= LANES and ssd_w % gn == 0
    off_b = 2 * ssd_w
    off_c = off_b + gn
    hpad = LANES - heads
    dtb = jnp.pad(dt_bias, (0, hpad)).reshape(1, LANES)
    alog = jnp.pad(a_log, (0, hpad)).reshape(1, LANES)
    dskip = jnp.repeat(d_skip, SSD_HEAD_DIM).reshape(1, ssd_w)
    hid = jnp.arange(LANES)
    e01 = (hid[:, None] == (jnp.arange(ssd_w) // SSD_HEAD_DIM)[None, :]).astype(BF16)
    sel = (hid[None, :, None] == (jnp.arange(g)[:, None, None] * r + hid[None, None, :])
           ) & (hid[None, None, :] < r)
    sel = sel.astype(BF16)
    row = lambda b, c: b * nc + c
    cw = conv_w
    cb2 = conv_b.reshape(1, -1)
    fixed = lambda shape, col: pl.BlockSpec(shape, lambda b, c: (0, col))
    in_specs = [
        pl.BlockSpec((L, ssd_w), lambda b, c: (row(b, c), 0)),
        pl.BlockSpec((L, ssd_w), lambda b, c: (row(b, c), 1)),
        pl.BlockSpec((L, gn), lambda b, c: (row(b, c), off_b // gn)),
        pl.BlockSpec((L, gn), lambda b, c: (row(b, c), off_c // gn)),
        pl.BlockSpec((L, LANES), lambda b, c: (row(b, c), 0)),
        fixed((SSD_CONV, ssd_w), 0),
        fixed((SSD_CONV, gn), ssd_w // gn),
        fixed((SSD_CONV, gn), ssd_w // gn + 1),
        fixed((1, ssd_w), 0),
        fixed((1, gn), ssd_w // gn),
        fixed((1, gn), ssd_w // gn + 1),
        fixed((1, LANES), 0),
        fixed((1, LANES), 0),
        fixed((1, ssd_w), 0),
        fixed((1, ssd_w), 0),
        fixed((LANES, ssd_w), 0),
        pl.BlockSpec((g, LANES, LANES), lambda b, c: (0, 0, 0)),
    ]
    return pl.pallas_call(
        functools.partial(_ssd_kernel, heads_per_group=r),
        grid=(bsz, nc),
        in_specs=in_specs,
        out_specs=pl.BlockSpec((L, ssd_w), lambda b, c: (row(b, c), 0)),
        out_shape=jax.ShapeDtypeStruct((t, ssd_w), BF16),
        scratch_shapes=[pltpu.VMEM((g, n, gw), F32),
                        pltpu.VMEM((L + 8, ssd_w), F32),
                        pltpu.VMEM((L + 8, gn), F32),
                        pltpu.VMEM((L + 8, gn), F32)],
        compiler_params=_cp(("arbitrary", "arbitrary")),
    )(proj, proj, proj, proj, dt_raw, cw, cw, cw, cb2, cb2, cb2,
      dtb, alog, dskip, norm_w.reshape(1, ssd_w), e01, sel)


def _hgrn_kernel(q_ref, f_ref, v_ref, g_ref, lb_ref, nw_ref, o_ref, st_ref):
    L = HG_L
    SB = HG_SUB
    nsb = L // SB
    rows = q_ref.shape[0]
    ci = pl.program_id(2)

    @pl.when(ci == 0)
    def _():
        st_ref[...] = jnp.zeros_like(st_ref)

    lb = lb_ref[...]
    nw = nw_ref[...]
    tri = _tri(L).astype(BF16)
    ones = jnp.ones((LANES, LANES), BF16)
    zpad = jnp.zeros((LANES - L, LANES), F32)
    lane = lax.broadcasted_iota(I32, (SB, LANES), 1)
    rowi = lax.broadcasted_iota(I32, (SB, LANES), 0)

    def log_decay(fr):
        return jnp.log(jnp.maximum(lb + (1.0 - lb) * jax.nn.sigmoid(fr), TINY))

    sub_decay = jnp.sum((-log_decay(f_ref[...])).reshape(rows // SB, SB, LANES), axis=1)
    small_decay = jnp.max(sub_decay) < HG_SAFE_DECAY

    def scores_factored(q, key, bcum):
        att_rows = []
        for i in range(nsb):
            lo, hi = i * SB, (i + 1) * SB
            edge = bcum[lo - 1:lo] if i > 0 else jnp.zeros((1, LANES), F32)
            qs = (q[lo:hi] * jnp.exp(bcum[lo:hi] - edge)).astype(BF16)
            ks = key[:hi] * jnp.exp(edge - bcum[:hi])
            ks = jnp.concatenate([ks, jnp.zeros((LANES - hi, LANES), F32)], axis=0).astype(BF16)
            att_rows.append(jnp.where(rowi + lo >= lane, _dot_nt(qs, ks), 0.0))
        return jnp.concatenate(att_rows, axis=0)

    def scores_exact(q, key, bcum):
        pieces = []
        for i in range(nsb):
            qi = q[i * SB:(i + 1) * SB]
            bi = bcum[i * SB:(i + 1) * SB]
            for j in range(SB):
                r = i * SB + j
                pieces.append((qi * key[r:r + 1] * jnp.exp(jnp.minimum(bi - bcum[r:r + 1], 0.0))).astype(BF16))
        sums = _dot(jnp.concatenate(pieces, axis=0), ones)

        att_rows = []
        for i in range(nsb):
            diag = jnp.zeros((SB, LANES), F32)
            for j in range(SB):
                r = i * SB + j
                diag = jnp.where(lane == r, sums[r * SB:(r + 1) * SB], diag)
            att_i = jnp.where(rowi + i * SB >= lane, diag, 0.0)
            if i > 0:
                edge = bcum[i * SB - 1:i * SB]
                qs = (q[i * SB:(i + 1) * SB] * jnp.exp(bcum[i * SB:(i + 1) * SB] - edge)).astype(BF16)
                ks = jnp.concatenate([key * jnp.exp(jnp.minimum(edge - bcum, 0.0)), zpad], axis=0).astype(BF16)
                att_i = jnp.where(lane < i * SB, _dot_nt(qs, ks), att_i)
            att_rows.append(att_i)
        return jnp.concatenate(att_rows, axis=0)

    def chunk(c, scores):
        r0 = c * L
        q = q_ref[r0:r0 + L, :]
        fr = f_ref[r0:r0 + L, :]
        v = v_ref[r0:r0 + L, :]
        log_f = log_decay(fr)
        key = (1.0 - lb) * jax.nn.sigmoid(-fr)
        bcum = _sel_left(tri, log_f)
        last = bcum[L - 1:L, :]
        att = scores(q, key, bcum)

        vpad = jnp.concatenate([v, zpad], axis=0)
        st = st_ref[...]
        o = (_dot_nt((q * jnp.exp(bcum)).astype(BF16), st.astype(BF16))
             + _dot(att.astype(BF16), vpad.astype(BF16)))
        kk = jnp.concatenate([key * jnp.exp(last - bcum), zpad], axis=0)
        st_ref[...] = st * jnp.exp(last) + _dot(vpad.T.astype(BF16), kk.astype(BF16))

        o = o * lax.rsqrt(jnp.mean(jnp.square(o), axis=-1, keepdims=True) + NORM_EPS) * nw
        o_ref[r0:r0 + L, :] = (o * _silu(g_ref[r0:r0 + L, :])).astype(o_ref.dtype)

    @pl.when(small_decay)
    def _():
        for c in range(rows // L):
            chunk(c, scores_factored)

    @pl.when(jnp.logical_not(small_decay))
    def _():
        for c in range(rows // L):
            chunk(c, scores_exact)


def _hgrn(proj, lower_bound, norm_w, bsz, seq, hg_w, off_q):
    t = proj.shape[0]
    rows = _pick(seq, (8 * HG_L, 4 * HG_L, 2 * HG_L, HG_L))
    nc = seq // rows
    h = hg_w // HEAD_V
    cq = off_q // LANES
    row = lambda b, hi, c: b * nc + c
    blk = lambda k: pl.BlockSpec((rows, LANES), lambda b, hi, c: (row(b, hi, c), cq + k * h + hi))
    vec = pl.BlockSpec((1, LANES), lambda b, hi, c: (0, hi))
    return pl.pallas_call(
        _hgrn_kernel,
        grid=(bsz, h, nc),
        in_specs=[blk(0), blk(1), blk(2), blk(3), vec, vec],
        out_specs=pl.BlockSpec((rows, LANES), lambda b, hi, c: (row(b, hi, c), hi)),
        out_shape=jax.ShapeDtypeStruct((t, hg_w), BF16),
        scratch_shapes=[pltpu.VMEM((HEAD_V, LANES), F32)],
        compiler_params=_cp(("arbitrary", "arbitrary", "arbitrary")),
    )(proj, proj, proj, proj, lower_bound.reshape(1, hg_w), norm_w.reshape(1, hg_w))


def _rope_kernel(pos_ref, freq_ref, cos_ref, sin_ref):
    ang = pos_ref[...].astype(F32) * freq_ref[...]
    lane = lax.broadcasted_iota(I32, ang.shape, 1)
    first = (lane % RET_KDIM) < (RET_KDIM // 2)
    cos_ref[...] = jnp.cos(ang)
    s = jnp.sin(ang)
    sin_ref[...] = jnp.where(first, -s, s)


def _rope_tables(positions):
    t = positions.size
    half = RET_KDIM // 2
    freq = ROPE_BASE ** (-jnp.arange(half, dtype=F32) / half)
    freq = jnp.tile(freq, LANES // half).reshape(1, LANES)
    tm = _pick(t, (512, 256, 128))
    return pl.pallas_call(
        _rope_kernel,
        grid=(t // tm,),
        in_specs=[pl.BlockSpec((tm, 1), lambda i: (i, 0)),
                  pl.BlockSpec((1, LANES), lambda i: (0, 0))],
        out_specs=[pl.BlockSpec((tm, LANES), lambda i: (i, 0))] * 2,
        out_shape=[jax.ShapeDtypeStruct((t, LANES), F32)] * 2,
        compiler_params=_cp(("arbitrary",)),
    )(positions.reshape(t, 1), freq)


def _ret_kernel(q_ref, k_ref, v_ref, g_ref, cos_ref, sin_ref, dm_ref, te_ref, fs_ref, cd_ref,
                nw_ref, nb_ref, o_ref, st_ref):
    L = RET_L
    n_pairs = te_ref.shape[0]
    ci = pl.program_id(2)

    @pl.when(ci == 0)
    def _():
        st_ref[...] = jnp.zeros_like(st_ref)

    cos = cos_ref[...]
    sin = sin_ref[...]
    lane = lax.broadcasted_iota(I32, (L, LANES), 1)
    rowi = lax.broadcasted_iota(I32, (L, LANES), 0)
    first = (lane % RET_KDIM) < (RET_KDIM // 2)

    def rope(x):
        sw = jnp.where(first, pltpu.roll(x, LANES - RET_KDIM // 2, 1), pltpu.roll(x, RET_KDIM // 2, 1))
        return x * cos + sw * sin

    for p in range(n_pairs):
        qr = rope(q_ref[:, p * LANES:(p + 1) * LANES])
        kr = rope(k_ref[:, p * LANES:(p + 1) * LANES]) * RET_KDIM ** -0.5
        kr16 = kr.astype(BF16)
        v16 = v_ref[:, 2 * p * HEAD_V:2 * (p + 1) * HEAD_V].astype(BF16)

        kt = (kr * te_ref[p]).T
        cd = cd_ref[p]

        def summary(frames):
            full = _dot(jnp.where(frames, kt, 0.0).astype(BF16), v16)
            return jnp.where(rowi < RET_KDIM, full[:, :HEAD_V], full[:, HEAD_V:])
        s0 = st_ref[p]
        s1 = s0 * cd + summary(lane < CHUNK)
        st_ref[p] = s1 * cd + summary(lane >= CHUNK)
        s0_16 = s0.astype(BF16)
        s1_16 = s1.astype(BF16)

        for hh in range(2):
            head = 2 * p + hh
            qh = jnp.where((lane // RET_KDIM) == hh, qr, 0.0).astype(BF16)
            scores = _dot_nt(qh, kr16) * dm_ref[head]
            o = _dot(scores.astype(BF16), v16[:, hh * HEAD_V:(hh + 1) * HEAD_V])
            inter = jnp.where(rowi < CHUNK, _dot(qh, s0_16), _dot(qh, s1_16))
            o = o + inter * fs_ref[head]
            mu = jnp.mean(o, axis=-1, keepdims=True)
            var = jnp.mean(jnp.square(o - mu), axis=-1, keepdims=True)
            sl = slice(head * HEAD_V, (head + 1) * HEAD_V)
            o = (o - mu) * lax.rsqrt(var + NORM_EPS) * nw_ref[:, sl] + nb_ref[:, sl]
            o_ref[:, sl] = (o * _silu(g_ref[:, sl])).astype(o_ref.dtype)


def _retention(proj, cos, sin, norm_w, norm_b, bsz, seq, ret_w, off_q):
    t = proj.shape[0]
    L = RET_L
    nc = seq // L
    h = ret_w // HEAD_V
    pairs = h // 2
    assert h % 2 == 0 and seq % L == 0
    kw = h * RET_KDIM
    off_k = off_q + kw
    off_v = off_k + kw
    off_g = off_v + ret_w
    assert off_v % (2 * HEAD_V) == 0 and off_g % (2 * HEAD_V) == 0
    log_gamma = jnp.log1p(-jnp.exp2(-5.0 - jnp.arange(h, dtype=F32)))
    pos = jnp.arange(CHUNK, dtype=F32)
    d_intra = jnp.exp(log_gamma[:, None, None] * jnp.abs(pos[:, None] - pos[None, :]))
    dm = jnp.zeros((h, L, L), F32)
    dm = dm.at[:, :CHUNK, :CHUNK].set(d_intra).at[:, CHUNK:, CHUNK:].set(d_intra)
    to_end = jnp.exp(log_gamma[:, None] * (CHUNK - 1 - pos))
    te = jnp.repeat(jnp.tile(to_end, (1, 2)).reshape(pairs, 2, L), RET_KDIM, axis=1)
    te = jnp.swapaxes(te, 1, 2)
    from_start = jnp.exp(log_gamma[:, None] * (pos + 1.0))
    fs = jnp.broadcast_to(jnp.tile(from_start, (1, 2))[:, :, None], (h, L, HEAD_V))
    cdec = jnp.exp(log_gamma * CHUNK)
    cd = jnp.broadcast_to(jnp.repeat(cdec.reshape(pairs, 2), RET_KDIM, axis=1)[:, :, None],
                          (pairs, 2 * RET_KDIM, HEAD_V))
    n_p = _pick(pairs, (4, 2, 1))
    wq = n_p * LANES
    wv = n_p * 2 * HEAD_V
    assert off_q % wq == 0 and off_k % wq == 0 and off_v % wv == 0 and off_g % wv == 0
    row = lambda b, p, c: b * nc + c
    tab = lambda n0, r: pl.BlockSpec((n0, r, LANES), lambda b, p, c: (p, 0, 0))
    return pl.pallas_call(
        _ret_kernel,
        grid=(bsz, pairs // n_p, nc),
        in_specs=[pl.BlockSpec((L, wq), lambda b, p, c: (row(b, p, c), off_q // wq + p)),
                  pl.BlockSpec((L, wq), lambda b, p, c: (row(b, p, c), off_k // wq + p)),
                  pl.BlockSpec((L, wv), lambda b, p, c: (row(b, p, c), off_v // wv + p)),
                  pl.BlockSpec((L, wv), lambda b, p, c: (row(b, p, c), off_g // wv + p)),
                  pl.BlockSpec((L, LANES), lambda b, p, c: (row(b, p, c), 0)),
                  pl.BlockSpec((L, LANES), lambda b, p, c: (row(b, p, c), 0)),
                  tab(2 * n_p, L),
                  tab(n_p, L),
                  tab(2 * n_p, L),
                  tab(n_p, 2 * RET_KDIM),
                  pl.BlockSpec((1, wv), lambda b, p, c: (0, p)),
                  pl.BlockSpec((1, wv), lambda b, p, c: (0, p))],
        out_specs=pl.BlockSpec((L, wv), lambda b, p, c: (row(b, p, c), p)),
        out_shape=jax.ShapeDtypeStruct((t, ret_w), BF16),
        scratch_shapes=[pltpu.VMEM((n_p, 2 * RET_KDIM, HEAD_V), F32)],
        compiler_params=_cp(("arbitrary", "arbitrary", "arbitrary")),
    )(proj, proj, proj, proj, cos, sin, dm, te, fs, cd,
      norm_w.reshape(1, ret_w), norm_b.reshape(1, ret_w))


def _layer_norm_rows(v, g, b):
    mu = jnp.mean(v, axis=-1, keepdims=True)
    var = jnp.mean(jnp.square(v - mu), axis=-1, keepdims=True)
    return (v - mu) * lax.rsqrt(var + NORM_EPS) * g + b


def _route_rows(p):
    rows = [p[e:e + 1, :] for e in range(N_EXPERTS)]
    scores = []
    for gi in range(N_EXPERT_GROUPS):
        a, b, c, d = rows[gi * EXPERTS_PER_GROUP:(gi + 1) * EXPERTS_PER_GROUP]
        hi1, lo1 = jnp.maximum(a, b), jnp.minimum(a, b)
        hi2, lo2 = jnp.maximum(c, d), jnp.minimum(c, d)
        scores.append(jnp.maximum(hi1, hi2) + jnp.maximum(jnp.minimum(hi1, hi2), jnp.maximum(lo1, lo2)))
    best = scores[0]
    sel = jnp.zeros_like(best, dtype=I32)
    for gi in range(1, N_EXPERT_GROUPS):
        better = scores[gi] > best
        best = jnp.where(better, scores[gi], best)
        sel = jnp.where(better, gi, sel)
    cand = []
    for e in range(EXPERTS_PER_GROUP):
        v = rows[e]
        for gi in range(1, N_EXPERT_GROUPS):
            v = jnp.where(sel == gi, rows[gi * EXPERTS_PER_GROUP + e], v)
        cand.append(v)

    def argbest(vals):
        bv = vals[0]
        bi = jnp.zeros_like(sel)
        for e in range(1, len(vals)):
            better = vals[e] > bv
            bv = jnp.where(better, vals[e], bv)
            bi = jnp.where(better, e, bi)
        return bv, bi
    w0, i0 = argbest(cand)
    w1, i1 = argbest([jnp.where(i0 == e, -1.0, cand[e]) for e in range(EXPERTS_PER_GROUP)])
    tot = w0 + w1
    base = sel * EXPERTS_PER_GROUP
    return base + i0, base + i1, w0 / tot, w1 / tot


def _row_to_vmem(hbm, row, buf, r, sem):
    d = buf.shape[1]
    return pltpu.make_async_copy(hbm.at[pl.ds(pl.multiple_of(row * d, d), d)], buf.at[r], sem)


def _row_to_hbm(buf, r, hbm, row, sem):
    d = buf.shape[1]
    return pltpu.make_async_copy(buf.at[r], hbm.at[pl.ds(pl.multiple_of(row * d, d), d)], sem)


def _ln_kernel(*refs, alpha, second_rows, emit_h, h_dtype, with_router):
    pair_rows = second_rows is not None
    it = iter(refs)
    x_ref, y_ref = next(it), next(it)
    if pair_rows:
        w_ref = next(it)
    gate_ref, g_ref, b_ref = next(it), next(it), next(it)
    if emit_h:
        sc_ref, sh_ref = next(it), next(it)
    if with_router:
        rwh_ref, rwl_ref, rb_ref = next(it), next(it), next(it)
    xo_ref = next(it)
    if emit_h:
        h_ref = next(it)
    if with_router:
        idx_ref, wt_ref, cnt_ref = next(it), next(it), next(it)
        hbufs, hsem = (next(it), next(it)), next(it)
    if pair_rows:
        ybufs, ysem = ((next(it), next(it)), (next(it), next(it))), next(it)
    tm = x_ref.shape[0]
    i = pl.program_id(0)
    n = pl.num_programs(0)

    def finish(y):
        xn = _layer_norm_rows(alpha * x_ref[...] + (1.0 + gate_ref[...]) * y, g_ref[...], b_ref[...])
        xo_ref[...] = xn
        if emit_h:
            h = xn * (1.0 + sc_ref[...]) + sh_ref[...]
            if not with_router:
                h_ref[...] = h.astype(h_dtype)
            return h

    if pair_rows:
        def fetch(tile, s, r):
            _row_to_vmem(y_ref, tile * tm + r, ybufs[s][0], r, ysem.at[s]).start()
            _row_to_vmem(y_ref, second_rows + tile * tm + r, ybufs[s][1], r, ysem.at[s]).start()

        def wait_tile(s):
            for r in range(tm):
                _row_to_vmem(y_ref, 0, ybufs[s][0], r, ysem.at[s]).wait()
                _row_to_vmem(y_ref, 0, ybufs[s][1], r, ysem.at[s]).wait()

        @pl.when(i == 0)
        def _():
            def first(r, c):
                fetch(0, 0, r)
                return c
            lax.fori_loop(0, tm, first, 0, unroll=8)

        def step(s):
            wait_tile(s)
            nxt = jnp.minimum(i + 1, n - 1)
            for r in range(tm):
                fetch(nxt, 1 - s, r)
            finish(ybufs[s][0][...] * w_ref[:, 0:1] + ybufs[s][1][...] * w_ref[:, 1:2])

            @pl.when(i == n - 1)
            def _():
                wait_tile(1 - s)
        for parity in range(2):
            pl.when(i % 2 == parity)(functools.partial(step, parity))
        return

    h = finish(y_ref[...])
    if with_router:
        h_hi = h.astype(BF16)
        h_lo = (h - h_hi.astype(F32)).astype(BF16)
        rwh = rwh_ref[...]
        logits = (_dot_nt(rwh, h_hi) + _dot_nt(rwh, h_lo) + _dot_nt(rwl_ref[...], h_hi)
                  + rb_ref[...])
        e = jnp.exp(logits - jnp.max(logits, axis=0, keepdims=True))
        p = e / jnp.sum(e, axis=0, keepdims=True)
        i0, i1, w0, w1 = _route_rows(p)

        @pl.when(i == 0)
        def _():
            cnt_ref[...] = jnp.zeros_like(cnt_ref)
        erow = lax.broadcasted_iota(I32, logits.shape, 0)
        oh0 = (erow == i0).astype(F32)
        oh1 = (erow == i1).astype(F32)
        both = oh0 + oh1
        tr = lax.broadcasted_iota(I32, (tm, tm), 0)
        tc = lax.broadcasted_iota(I32, (tm, tm), 1)
        before = cnt_ref[:, 0:1] + _dot(both.astype(BF16), (tr < tc).astype(BF16))
        r0 = jnp.sum(oh0 * before, axis=0, keepdims=True).astype(I32)
        r1 = jnp.sum(oh1 * before, axis=0, keepdims=True).astype(I32)
        cnt_ref[...] = cnt_ref[...] + jnp.sum(both, axis=1, keepdims=True)
        zi = jnp.zeros((4, tm), I32)
        idx_ref[...] = jnp.concatenate([i0, i1, r0, r1, zi], axis=0)
        wt_ref[...] = jnp.concatenate([w0, w1, jnp.zeros((6, tm), F32)], axis=0)

        def wait_sent(s):
            for r in range(tm):
                _row_to_hbm(hbufs[s], r, h_ref, 0, hsem.at[s]).wait()

        def send(s):
            @pl.when(i >= 2)
            def _():
                wait_sent(s)
            hbufs[s][...] = h
            for r in range(tm):
                _row_to_hbm(hbufs[s], r, h_ref, i * tm + r, hsem.at[s]).start()

            @pl.when(i == n - 1)
            def _():
                wait_sent(s)
                wait_sent(1 - s)
        for parity in range(2):
            pl.when(i % 2 == parity)(functools.partial(send, parity))


def _post_norm(x2, y, gate, ln_g, ln_b, alpha, seq, *, second_rows=None, pair_w=None, scale=None,
               shift=None, h_dtype=BF16, router=None):
    t, d = x2.shape
    tm = _pick(seq, (128, 64))
    emit_h = scale is not None
    with_router = router is not None
    pair_rows = second_rows is not None
    assert t // tm >= 2
    vec = pl.BlockSpec((None, 1, d), lambda i: (i * tm // seq, 0, 0))
    par = pl.BlockSpec((1, d), lambda i: (0, 0))
    rows = pl.BlockSpec((tm, d), lambda i: (i, 0))
    hbm = pl.BlockSpec(memory_space=pl.ANY)
    in_specs = [rows, hbm if pair_rows else rows]
    args = [x2, y]
    if pair_rows:
        in_specs.append(pl.BlockSpec((tm, TOP_K), lambda i: (i, 0)))
        args.append(pair_w)
    in_specs += [vec, par, par]
    args += [gate, ln_g.reshape(1, d), ln_b.reshape(1, d)]
    out_specs = [rows]
    out_shape = [jax.ShapeDtypeStruct((t, d), F32)]
    scratch = []
    if emit_h:
        in_specs += [vec, vec]
        args += [scale, shift]
        if with_router:
            out_specs.append(hbm)
            out_shape.append(jax.ShapeDtypeStruct((t * d,), h_dtype))
        else:
            out_specs.append(rows)
            out_shape.append(jax.ShapeDtypeStruct((t, d), h_dtype))
    if with_router:
        rwh, rwl, rb = router
        e = rwh.shape[0]
        in_specs += [pl.BlockSpec((e, d), lambda i: (0, 0)), pl.BlockSpec((e, d), lambda i: (0, 0)),
                     pl.BlockSpec((e, 1), lambda i: (0, 0))]
        args += [rwh, rwl, rb]
        out_specs += [pl.BlockSpec((8, tm), lambda i: (0, i))] * 2
        out_shape += [jax.ShapeDtypeStruct((8, t), I32), jax.ShapeDtypeStruct((8, t), F32)]
        scratch += [pltpu.VMEM((N_EXPERTS, LANES), F32), pltpu.VMEM((tm, d), h_dtype),
                    pltpu.VMEM((tm, d), h_dtype), pltpu.SemaphoreType.DMA((2,))]
    if pair_rows:
        scratch += [pltpu.VMEM((tm, d), F32)] * 4 + [pltpu.SemaphoreType.DMA((2,))]
    return pl.pallas_call(
        functools.partial(_ln_kernel, alpha=alpha, second_rows=second_rows, emit_h=emit_h,
                          h_dtype=h_dtype, with_router=with_router),
        grid=(t // tm,),
        in_specs=in_specs, out_specs=out_specs, out_shape=out_shape, scratch_shapes=scratch,
        compiler_params=_cp(("arbitrary",)),
    )(*args)


def _moe_kernel(src_ref, dst_ref, exp_ref, h_hbm, wg_ref, wu_ref, wd_ref, y_hbm,
                xbuf0, xbuf1, ybuf0, ybuf1, gsem, ssem):
    i = pl.program_id(0)
    nb = pl.num_programs(0)
    R = MOE_BLOCK
    xbufs = (xbuf0, xbuf1)
    ybufs = (ybuf0, ybuf1)

    def gather_row(block, s, r):
        _row_to_vmem(h_hbm, src_ref[block * R + r], xbufs[s], r, gsem.at[s]).start()

    def scatter_row(ext_block, s, r):
        _row_to_hbm(ybufs[s], r, y_hbm, dst_ref[ext_block * R + r], ssem.at[s]).start()

    def wait_gather(s):
        for r in range(R):
            _row_to_vmem(h_hbm, 0, xbufs[s], r, gsem.at[s]).wait()

    def wait_scatter(s):
        for r in range(R):
            _row_to_hbm(ybufs[s], r, y_hbm, 0, ssem.at[s]).wait()

    @pl.when(i == 0)
    def _():
        ybuf1[...] = jnp.zeros(ybuf1.shape, F32)

        def first(r, c):
            gather_row(0, 0, r)
            return c
        lax.fori_loop(0, R, first, 0, unroll=8)

    def step(slot):
        other = 1 - slot
        wait_gather(slot)
        nxt = jnp.minimum(i + 1, nb - 1)
        x = xbufs[slot][...].astype(BF16)
        hid = _silu(_dot(x, wg_ref[...])) * _dot(x, wu_ref[...])
        yb = _dot(hid.astype(BF16), wd_ref[...])
        for r in range(R):
            gather_row(nxt, other, r)
            scatter_row(i, other, r)

        @pl.when(i >= 1)
        def _():
            wait_scatter(slot)
        ybufs[slot][...] = yb

        @pl.when(i == nb - 1)
        def _():
            def last(r, c):
                scatter_row(i + 1, slot, r)
                return c
            lax.fori_loop(0, R, last, 0, unroll=8)
            wait_gather(other)
            wait_scatter(other)
            wait_scatter(slot)

    for parity in range(2):
        pl.when(i % 2 == parity)(functools.partial(step, parity))


def _moe_blocks(h_flat, slot_src, slot_dst, block_exp, wg, wu, wd, n_rows_out):
    d = wg.shape[1]
    f = wg.shape[2]
    n_blocks = block_exp.shape[0]
    R = MOE_BLOCK
    dst_ext = jnp.concatenate([n_rows_out + jnp.arange(R, dtype=I32), slot_dst])
    grid_spec = pltpu.PrefetchScalarGridSpec(
        num_scalar_prefetch=3,
        grid=(n_blocks,),
        in_specs=[pl.BlockSpec(memory_space=pl.ANY),
                  pl.BlockSpec((None, d, f), lambda i, s, dd, e: (e[i], 0, 0)),
                  pl.BlockSpec((None, d, f), lambda i, s, dd, e: (e[i], 0, 0)),
                  pl.BlockSpec((None, f, d), lambda i, s, dd, e: (e[i], 0, 0))],
        out_specs=pl.BlockSpec(memory_space=pl.ANY),
        scratch_shapes=[pltpu.VMEM((R, d), F32), pltpu.VMEM((R, d), F32),
                        pltpu.VMEM((R, d), F32), pltpu.VMEM((R, d), F32),
                        pltpu.SemaphoreType.DMA((2,)),
                        pltpu.SemaphoreType.DMA((2,))],
    )
    return pl.pallas_call(
        _moe_kernel,
        grid_spec=grid_spec,
        out_shape=jax.ShapeDtypeStruct(((n_rows_out + R) * d,), F32),
        compiler_params=_cp(("arbitrary",)),
    )(slot_src, dst_ext, block_exp, h_flat, wg, wu, wd)


def _dispatch_plan(idx, t):
    n_assign = t * TOP_K
    e01 = idx[:TOP_K]
    rank01 = idx[TOP_K:2 * TOP_K]
    experts = jnp.arange(N_EXPERTS, dtype=I32)
    onehot = (e01[:, :, None] == experts).astype(I32)
    counts = jnp.sum(onehot, axis=(0, 1))
    padded = (counts + MOE_BLOCK - 1) // MOE_BLOCK * MOE_BLOCK
    pad_end = jnp.cumsum(padded)
    pad_start = pad_end - padded
    real_end = jnp.cumsum(counts)
    dest = jnp.sum(onehot * pad_start, axis=-1) + rank01
    n_blocks = -(-n_assign // MOE_BLOCK) + N_EXPERTS
    n_slots = n_blocks * MOE_BLOCK
    block_start = jnp.arange(n_blocks, dtype=I32) * MOE_BLOCK
    block_exp = jnp.minimum(jnp.sum(block_start[:, None] >= pad_end[None, :], -1), N_EXPERTS - 1).astype(I32)
    second = t + n_slots - n_assign
    tok = jnp.arange(t, dtype=I32)
    out_row = jnp.stack([tok, second + tok])
    slot_row = jnp.full((n_slots,), -1, I32).at[dest.reshape(-1)].set(out_row.reshape(-1))
    s = jnp.arange(n_slots, dtype=I32)
    pad_rank = s - jnp.repeat(real_end[block_exp], MOE_BLOCK)
    is_real = slot_row >= 0
    slot_dst = jnp.where(is_real, slot_row, t + pad_rank)
    slot_src = jnp.where(is_real, jnp.where(slot_row >= second, slot_row - second, slot_row), 0)
    return slot_src, slot_dst, block_exp, second


def kernel(x, c, positions, w_in, conv_w, conv_b, dt_bias, a_log, d_skip, ssd_norm_w, hgrn_gamma,
           hgrn_norm_w, ret_norm_w, ret_norm_b, w_out, ada_down, ada_up, ada_b, ln_g, ln_b,
           router_w, router_b, w_gate, w_up, w_down):
    bsz, seq, d = x.shape
    depth = w_in.shape[0]
    t = bsz * seq
    ssd_w = d // 2
    ssd_heads = ssd_w // SSD_HEAD_DIM
    conv_dim = ssd_w + 2 * SSD_GROUPS * SSD_STATE
    hg_w = d // 4
    ret_w = d // 4
    ret_kw = (ret_w // HEAD_V) * RET_KDIM
    alpha = (2.0 * depth) ** 0.25
    off_dt = ssd_w + conv_dim
    off_hq = off_dt
    off_rq = off_hq + 4 * hg_w

    p = jax.nn.softmax(hgrn_gamma.astype(F32), axis=0)
    lower_bounds = jnp.cumsum(p, axis=0) - p[0]
    mod = _modulation(c, ada_down, ada_up, ada_b)
    cos, sin = _rope_tables(positions)

    rwt = router_w.T
    rw_hi = rwt.astype(BF16)
    rw_lo = (rwt - rw_hi.astype(F32)).astype(BF16)
    rb = router_b.reshape(N_EXPERTS, 1).astype(F32)

    x2 = x.reshape(t, d)
    h = _modulate(x2, mod[0, 1], mod[0, 0], seq)
    for l in range(depth):
        shift1, scale1, gate1, shift2, scale2, gate2 = [mod[l, i] for i in range(N_MOD)]
        w_main = jnp.concatenate([w_in[l][:, :off_dt], w_in[l][:, off_dt + ssd_heads:]], axis=1).astype(BF16)
        w_dt = jnp.pad(w_in[l][:, off_dt:off_dt + ssd_heads], ((0, 0), (0, LANES - ssd_heads))).astype(BF16)
        proj = _matmul(h, w_main, F32)
        dt_raw = _matmul(h, w_dt, F32)
        y_ssd = _ssd(proj, dt_raw, conv_w[l], conv_b[l], dt_bias[l], a_log[l], d_skip[l], ssd_norm_w[l],
                     bsz, seq, ssd_w)
        y_hg = _hgrn(proj, lower_bounds[l], hgrn_norm_w[l], bsz, seq, hg_w, off_hq)
        y_ret = _retention(proj, cos, sin, ret_norm_w[l], ret_norm_b[l], bsz, seq, ret_w, off_rq)
        mixed = _out_proj(y_ssd, y_hg, y_ret, w_out[l].astype(BF16))
        x2, h2, idx, wts = _post_norm(x2, mixed, gate1, ln_g[l, 0], ln_b[l, 0], alpha, seq,
                                      scale=scale2, shift=shift2, h_dtype=F32, router=(rw_hi, rw_lo, rb))
        slot_src, slot_dst, block_exp, second = _dispatch_plan(idx, t)
        y2 = _moe_blocks(h2, slot_src, slot_dst, block_exp,
                         w_gate[l].astype(BF16), w_up[l].astype(BF16), w_down[l].astype(BF16), second + t)
        pair_w = wts[:TOP_K].T
        if l + 1 < depth:
            x2, h = _post_norm(x2, y2, gate2, ln_g[l, 1], ln_b[l, 1], alpha, seq, second_rows=second,
                               pair_w=pair_w, scale=mod[l + 1, 1], shift=mod[l + 1, 0])
        else:
            (x2,) = _post_norm(x2, y2, gate2, ln_g[l, 1], ln_b[l, 1], alpha, seq, second_rows=second,
                               pair_w=pair_w)
    return x2.reshape(bsz, seq, d)
```

```python
import functools
import math

import jax
import jax.numpy as jnp
import numpy as np
from jax import lax
from jax.experimental import pallas as pl
from jax.experimental.pallas import tpu as pltpu

F32 = jnp.float32
BF16 = jnp.bfloat16
I32 = jnp.int32

CHUNK = 64
TINY = 1e-30
NORM_EPS = 1e-5
ROPE_BASE = 10000.0
LANES = 128
SSD_HEAD_DIM = 64
SSD_GROUPS = 4
SSD_STATE = 128
SSD_CONV = 4
HEAD_V = 128
RET_KDIM = 64
N_EXPERTS = 16
N_EXPERT_GROUPS = 4
EXPERTS_PER_GROUP = N_EXPERTS // N_EXPERT_GROUPS
TOP_K = 2
MOE_BLOCK = 256
N_MOD = 6
VMEM_LIMIT = 56 * 1024 * 1024

SSD_L = 128
HG_L = 64
HG_SUB = 16
HG_SAFE_DECAY = 60.0
RET_L = 2 * CHUNK


def _cp(sem):
    return pltpu.CompilerParams(dimension_semantics=sem, vmem_limit_bytes=VMEM_LIMIT)


def _pick(n, cands):
    for c in cands:
        if n % c == 0:
            return c
    return n


def _silu(x):
    return x * jax.nn.sigmoid(x)


def _split3(v):
    hi = v.astype(BF16)
    r1 = v - hi.astype(F32)
    mid = r1.astype(BF16)
    lo = (r1 - mid.astype(F32)).astype(BF16)
    return hi, mid, lo


def _dot(a, b):
    return jnp.dot(a, b, preferred_element_type=F32)


def _dot_nt(a, b):
    return lax.dot_general(a, b, (((1,), (1,)), ((), ())), preferred_element_type=F32)


def _sel_right(v, m01):
    hi, mid, lo = _split3(v)
    return _dot(hi, m01) + _dot(mid, m01) + _dot(lo, m01)


def _sel_left(m01, v):
    hi, mid, lo = _split3(v)
    return _dot(m01, hi) + _dot(m01, mid) + _dot(m01, lo)


def _tri(n):
    r = lax.broadcasted_iota(I32, (n, n), 0)
    c = lax.broadcasted_iota(I32, (n, n), 1)
    return r >= c


def _mod_kernel(c_ref, down_ref, up_ref, b_ref, o_ref):
    c = c_ref[...]
    t = jnp.dot(_silu(c), down_ref[...], preferred_element_type=F32,
                precision=lax.Precision.HIGHEST)
    o_ref[...] = jnp.dot(t, up_ref[...], preferred_element_type=F32,
                         precision=lax.Precision.HIGHEST) + b_ref[...]


def _modulation(c, ada_down, ada_up, ada_b):
    depth, d, rank = ada_down.shape
    bsz = c.shape[0]
    rows = 8
    cp = jnp.zeros((rows, d), F32).at[:bsz].set(c)
    out = pl.pallas_call(
        _mod_kernel,
        grid=(depth, N_MOD),
        in_specs=[pl.BlockSpec((rows, d), lambda l, j: (0, 0)),
                  pl.BlockSpec((None, d, rank), lambda l, j: (l, 0, 0)),
                  pl.BlockSpec((None, rank, d), lambda l, j: (l, 0, j)),
                  pl.BlockSpec((None, 1, d), lambda l, j: (l, 0, j))],
        out_specs=pl.BlockSpec((None, None, rows, d), lambda l, j: (l, j, 0, 0)),
        out_shape=jax.ShapeDtypeStruct((depth, N_MOD, rows, d), F32),
        compiler_params=_cp(("arbitrary", "arbitrary")),
    )(cp, ada_down, ada_up, ada_b.reshape(depth, 1, N_MOD * d))
    return out[:, :, :bsz, None, :]


def _modulate_kernel(x_ref, sc_ref, sh_ref, h_ref):
    h_ref[...] = (x_ref[...] * (1.0 + sc_ref[...]) + sh_ref[...]).astype(h_ref.dtype)


def _modulate(x2, scale, shift, seq):
    t, d = x2.shape
    tm = _pick(seq, (512, 256, 128, 64))
    vec = pl.BlockSpec((None, 1, d), lambda i: (i * tm // seq, 0, 0))
    return pl.pallas_call(
        _modulate_kernel,
        grid=(t // tm,),
        in_specs=[pl.BlockSpec((tm, d), lambda i: (i, 0)), vec, vec],
        out_specs=pl.BlockSpec((tm, d), lambda i: (i, 0)),
        out_shape=jax.ShapeDtypeStruct((t, d), BF16),
        compiler_params=_cp(("arbitrary",)),
    )(x2, scale, shift)


def _mm_kernel(a_ref, b_ref, o_ref):
    o_ref[...] = _dot(a_ref[...], b_ref[...]).astype(o_ref.dtype)


def _matmul(a, b, out_dtype):
    m, k = a.shape
    n = b.shape[1]
    tm = _pick(m, (1024, 512, 256, 128))
    tn = _pick(n, (1024, 768, 512, 384, 256, 128))
    return pl.pallas_call(
        _mm_kernel,
        grid=(m // tm, n // tn),
        in_specs=[pl.BlockSpec((tm, k), lambda i, j: (i, 0)),
                  pl.BlockSpec((k, tn), lambda i, j: (0, j))],
        out_specs=pl.BlockSpec((tm, tn), lambda i, j: (i, j)),
        out_shape=jax.ShapeDtypeStruct((m, n), out_dtype),
        compiler_params=_cp(("arbitrary", "arbitrary")),
    )(a, b)


def _mm3_kernel(a1_ref, a2_ref, a3_ref, b1_ref, b2_ref, b3_ref, o_ref):
    o_ref[...] = (_dot(a1_ref[...], b1_ref[...]) + _dot(a2_ref[...], b2_ref[...])
                  + _dot(a3_ref[...], b3_ref[...]))


def _out_proj(y_ssd, y_hg, y_ret, w_out):
    m = y_ssd.shape[0]
    d = w_out.shape[1]
    k1, k2, k3 = y_ssd.shape[1], y_hg.shape[1], y_ret.shape[1]
    tm = _pick(m, (1024, 512, 256, 128))
    tn = _pick(d, (1024, 512, 256, 128))
    return pl.pallas_call(
        _mm3_kernel,
        grid=(m // tm, d // tn),
        in_specs=[pl.BlockSpec((tm, k1), lambda i, j: (i, 0)),
                  pl.BlockSpec((tm, k2), lambda i, j: (i, 0)),
                  pl.BlockSpec((tm, k3), lambda i, j: (i, 0)),
                  pl.BlockSpec((k1, tn), lambda i, j: (0, j)),
                  pl.BlockSpec((k2, tn), lambda i, j: (k1 // k2, j)),
                  pl.BlockSpec((k3, tn), lambda i, j: ((k1 + k2) // k3, j))],
        out_specs=pl.BlockSpec((tm, tn), lambda i, j: (i, j)),
        out_shape=jax.ShapeDtypeStruct((m, d), F32),
        compiler_params=_cp(("arbitrary", "arbitrary")),
    )(y_ssd, y_hg, y_ret, w_out, w_out, w_out)


def _ssd_kernel(z_ref, xs_ref, b_ref, c_ref, dt_ref,
                wx_ref, wb_ref, wc_ref, bx_ref, bb_ref, bc_ref,
                dtb_ref, alog_ref, dskip_ref, nw_ref, e_ref, sel_ref,
                o_ref, st_ref, px_ref, pb_ref, pc_ref, *, heads_per_group):
    L = SSD_L
    n = SSD_STATE
    groups = sel_ref.shape[0]
    gw = o_ref.shape[1] // groups
    ci = pl.program_id(1)

    @pl.when(ci == 0)
    def _():
        st_ref[...] = jnp.zeros_like(st_ref)
        px_ref[0:8, :] = jnp.zeros((8, px_ref.shape[1]), F32)
        pb_ref[0:8, :] = jnp.zeros((8, pb_ref.shape[1]), F32)
        pc_ref[0:8, :] = jnp.zeros((8, pc_ref.shape[1]), F32)

    def conv_silu(raw_ref, pad_ref, w_ref, bias_ref):
        pad_ref[8:8 + L, :] = raw_ref[...]
        acc = bias_ref[...]
        for k in range(SSD_CONV):
            acc = acc + w_ref[k:k + 1, :] * pad_ref[pl.ds(8 - (SSD_CONV - 1) + k, L), :]
        pad_ref[0:8, :] = pad_ref[L:L + 8, :]
        return _silu(acc)

    xs_all = conv_silu(xs_ref, px_ref, wx_ref, bx_ref)
    bm_all = conv_silu(b_ref, pb_ref, wb_ref, bb_ref)
    cm_all = conv_silu(c_ref, pc_ref, wc_ref, bc_ref)

    dt = jax.nn.softplus(dt_ref[...] + dtb_ref[...])
    dta = dt * (-jnp.exp(alog_ref[...]))
    tri = _tri(L)
    cum = _sel_left(tri.astype(BF16), dta)
    e01 = e_ref[...]
    dt_full_all = _sel_right(dt, e01)
    cum_full_all = _sel_right(cum, e01)
    lane = lax.broadcasted_iota(I32, (L, LANES), 1)
    lo_half = lane < SSD_HEAD_DIM

    for g in range(groups):
        sl = slice(g * gw, (g + 1) * gw)
        xs = xs_all[:, sl]
        bm = bm_all[:, g * n:(g + 1) * n]
        cm = cm_all[:, g * n:(g + 1) * n]
        cum_g = _sel_right(cum, sel_ref[g])
        cum_gt = cum_g.T
        cum_full = cum_full_all[:, sl]
        last = cum_full[L - 1:L, :]
        to_end = jnp.exp(last - cum_full)
        from_start = jnp.exp(cum_full)
        xdt = xs * dt_full_all[:, sl]

        bm16 = bm.astype(BF16)
        cm16 = cm.astype(BF16)
        cb = _dot_nt(cm16, bm16)
        ys = []
        for j in range(heads_per_group // 2):
            atts = []
            for i in (2 * j, 2 * j + 1):
                dec = jnp.where(tri, jnp.exp(jnp.minimum(cum_g[:, i:i + 1] - cum_gt[i:i + 1, :], 0.0)), 0.0)
                atts.append((cb * dec).astype(BF16))
            xp = xdt[:, j * LANES:(j + 1) * LANES]
            stack = jnp.concatenate([jnp.where(lo_half, xp, 0.0), jnp.where(lo_half, 0.0, xp)], axis=0)
            ys.append(_dot(jnp.concatenate(atts, axis=1), stack.astype(BF16)))
        y = jnp.concatenate(ys, axis=1) if len(ys) > 1 else ys[0]

        st = st_ref[g]
        y = y + _dot(cm16, st.astype(BF16)) * from_start
        st_ref[g] = st * jnp.exp(last) + _dot(bm.T.astype(BF16), (xdt * to_end).astype(BF16))

        y = y + xs * dskip_ref[:, sl]
        y = y * _silu(z_ref[:, sl])
        y = y * lax.rsqrt(jnp.mean(jnp.square(y), axis=-1, keepdims=True) + NORM_EPS) * nw_ref[:, sl]
        o_ref[:, sl] = y.astype(o_ref.dtype)


def _ssd(proj, dt_raw, conv_w, conv_b, dt_bias, a_log, d_skip, norm_w, bsz, seq, ssd_w):
    t = proj.shape[0]
    L = SSD_L
    nc = seq // L
    g = SSD_GROUPS
    gw = ssd_w // g
    n = SSD_STATE
    gn = g * n
    heads = ssd_w // SSD_HEAD_DIM
    r = heads // g
    assert r % 2 == 0 and gw % LANES == 0 and seq % L == 0 and heads <= LANES and ssd_w % gn == 0
    off_b = 2 * ssd_w
    off_c = off_b + gn
    hpad = LANES - heads
    dtb = jnp.pad(dt_bias, (0, hpad)).reshape(1, LANES)
    alog = jnp.pad(a_log, (0, hpad)).reshape(1, LANES)
    dskip = jnp.repeat(d_skip, SSD_HEAD_DIM).reshape(1, ssd_w)
    hid = jnp.arange(LANES)
    e01 = (hid[:, None] == (jnp.arange(ssd_w) // SSD_HEAD_DIM)[None, :]).astype(BF16)
    sel = (hid[None, :, None] == (jnp.arange(g)[:, None, None] * r + hid[None, None, :])
           ) & (hid[None, None, :] < r)
    sel = sel.astype(BF16)
    row = lambda b, c: b * nc + c
    cw = conv_w
    cb2 = conv_b.reshape(1, -1)
    fixed = lambda shape, col: pl.BlockSpec(shape, lambda b, c: (0, col))
    in_specs = [
        pl.BlockSpec((L, ssd_w), lambda b, c: (row(b, c), 0)),
        pl.BlockSpec((L, ssd_w), lambda b, c: (row(b, c), 1)),
        pl.BlockSpec((L, gn), lambda b, c: (row(b, c), off_b // gn)),
        pl.BlockSpec((L, gn), lambda b, c: (row(b, c), off_c // gn)),
        pl.BlockSpec((L, LANES), lambda b, c: (row(b, c), 0)),
        fixed((SSD_CONV, ssd_w), 0),
        fixed((SSD_CONV, gn), ssd_w // gn),
        fixed((SSD_CONV, gn), ssd_w // gn + 1),
        fixed((1, ssd_w), 0),
        fixed((1, gn), ssd_w // gn),
        fixed((1, gn), ssd_w // gn + 1),
        fixed((1, LANES), 0),
        fixed((1, LANES), 0),
        fixed((1, ssd_w), 0),
        fixed((1, ssd_w), 0),
        fixed((LANES, ssd_w), 0),
        pl.BlockSpec((g, LANES, LANES), lambda b, c: (0, 0, 0)),
    ]
    return pl.pallas_call(
        functools.partial(_ssd_kernel, heads_per_group=r),
        grid=(bsz, nc),
        in_specs=in_specs,
        out_specs=pl.BlockSpec((L, ssd_w), lambda b, c: (row(b, c), 0)),
        out_shape=jax.ShapeDtypeStruct((t, ssd_w), BF16),
        scratch_shapes=[pltpu.VMEM((g, n, gw), F32),
                        pltpu.VMEM((L + 8, ssd_w), F32),
                        pltpu.VMEM((L + 8, gn), F32),
                        pltpu.VMEM((L + 8, gn), F32)],
        compiler_params=_cp(("arbitrary", "arbitrary")),
    )(proj, proj, proj, proj, dt_raw, cw, cw, cw, cb2, cb2, cb2,
      dtb, alog, dskip, norm_w.reshape(1, ssd_w), e01, sel)


def _hgrn_kernel(q_ref, f_ref, v_ref, g_ref, lb_ref, nw_ref, o_ref, st_ref):
    L = HG_L
    SB = HG_SUB
    nsb = L // SB
    rows = q_ref.shape[0]
    ci = pl.program_id(2)

    @pl.when(ci == 0)
    def _():
        st_ref[...] = jnp.zeros_like(st_ref)

    lb = lb_ref[...]
    nw = nw_ref[...]
    tri = _tri(L).astype(BF16)
    ones = jnp.ones((LANES, LANES), BF16)
    zpad = jnp.zeros((LANES - L, LANES), F32)
    lane = lax.broadcasted_iota(I32, (SB, LANES), 1)
    rowi = lax.broadcasted_iota(I32, (SB, LANES), 0)

    def log_decay(fr):
        return jnp.log(jnp.maximum(lb + (1.0 - lb) * jax.nn.sigmoid(fr), TINY))

    sub_decay = jnp.sum((-log_decay(f_ref[...])).reshape(rows // SB, SB, LANES), axis=1)
    small_decay = jnp.max(sub_decay) < HG_SAFE_DECAY

    def scores_factored(q, key, bcum):
        att_rows = []
        for i in range(nsb):
            lo, hi = i * SB, (i + 1) * SB
            edge = bcum[lo - 1:lo] if i > 0 else jnp.zeros((1, LANES), F32)
            qs = (q[lo:hi] * jnp.exp(bcum[lo:hi] - edge)).astype(BF16)
            ks = key[:hi] * jnp.exp(edge - bcum[:hi])
            ks = jnp.concatenate([ks, jnp.zeros((LANES - hi, LANES), F32)], axis=0).astype(BF16)
            att_rows.append(jnp.where(rowi + lo >= lane, _dot_nt(qs, ks), 0.0))
        return jnp.concatenate(att_rows, axis=0)

    def scores_exact(q, key, bcum):
        pieces = []
        for i in range(nsb):
            qi = q[i * SB:(i + 1) * SB]
            bi = bcum[i * SB:(i + 1) * SB]
            for j in range(SB):
                r = i * SB + j
                pieces.append((qi * key[r:r + 1] * jnp.exp(jnp.minimum(bi - bcum[r:r + 1], 0.0))).astype(BF16))
        sums = _dot(jnp.concatenate(pieces, axis=0), ones)

        att_rows = []
        for i in range(nsb):
            diag = jnp.zeros((SB, LANES), F32)
            for j in range(SB):
                r = i * SB + j
                diag = jnp.where(lane == r, sums[r * SB:(r + 1) * SB], diag)
            att_i = jnp.where(rowi + i * SB >= lane, diag, 0.0)
            if i > 0:
                edge = bcum[i * SB - 1:i * SB]
                qs = (q[i * SB:(i + 1) * SB] * jnp.exp(bcum[i * SB:(i + 1) * SB] - edge)).astype(BF16)
                ks = jnp.concatenate([key * jnp.exp(jnp.minimum(edge - bcum, 0.0)), zpad], axis=0).astype(BF16)
                att_i = jnp.where(lane < i * SB, _dot_nt(qs, ks), att_i)
            att_rows.append(att_i)
        return jnp.concatenate(att_rows, axis=0)

    def chunk(c, scores):
        r0 = c * L
        q = q_ref[r0:r0 + L, :]
        fr = f_ref[r0:r0 + L, :]
        v = v_ref[r0:r0 + L, :]
        log_f = log_decay(fr)
        key = (1.0 - lb) * jax.nn.sigmoid(-fr)
        bcum = _sel_left(tri, log_f)
        last = bcum[L - 1:L, :]
        att = scores(q, key, bcum)

        vpad = jnp.concatenate([v, zpad], axis=0)
        st = st_ref[...]
        o = (_dot_nt((q * jnp.exp(bcum)).astype(BF16), st.astype(BF16))
             + _dot(att.astype(BF16), vpad.astype(BF16)))
        kk = jnp.concatenate([key * jnp.exp(last - bcum), zpad], axis=0)
        st_ref[...] = st * jnp.exp(last) + _dot(vpad.T.astype(BF16), kk.astype(BF16))

        o = o * lax.rsqrt(jnp.mean(jnp.square(o), axis=-1, keepdims=True) + NORM_EPS) * nw
        o_ref[r0:r0 + L, :] = (o * _silu(g_ref[r0:r0 + L, :])).astype(o_ref.dtype)

    @pl.when(small_decay)
    def _():
        for c in range(rows // L):
            chunk(c, scores_factored)

    @pl.when(jnp.logical_not(small_decay))
    def _():
        for c in range(rows // L):
            chunk(c, scores_exact)


def _hgrn(proj, lower_bound, norm_w, bsz, seq, hg_w, off_q):
    t = proj.shape[0]
    rows = _pick(seq, (8 * HG_L, 4 * HG_L, 2 * HG_L, HG_L))
    nc = seq // rows
    h = hg_w // HEAD_V
    cq = off_q // LANES
    row = lambda b, hi, c: b * nc + c
    blk = lambda k: pl.BlockSpec((rows, LANES), lambda b, hi, c: (row(b, hi, c), cq + k * h + hi))
    vec = pl.BlockSpec((1, LANES), lambda b, hi, c: (0, hi))
    return pl.pallas_call(
        _hgrn_kernel,
        grid=(bsz, h, nc),
        in_specs=[blk(0), blk(1), blk(2), blk(3), vec, vec],
        out_specs=pl.BlockSpec((rows, LANES), lambda b, hi, c: (row(b, hi, c), hi)),
        out_shape=jax.ShapeDtypeStruct((t, hg_w), BF16),
        scratch_shapes=[pltpu.VMEM((HEAD_V, LANES), F32)],
        compiler_params=_cp(("arbitrary", "arbitrary", "arbitrary")),
    )(proj, proj, proj, proj, lower_bound.reshape(1, hg_w), norm_w.reshape(1, hg_w))


def _rope_kernel(pos_ref, freq_ref, cos_ref, sin_ref):
    ang = pos_ref[...].astype(F32) * freq_ref[...]
    lane = lax.broadcasted_iota(I32, ang.shape, 1)
    first = (lane % RET_KDIM) < (RET_KDIM // 2)
    cos_ref[...] = jnp.cos(ang)
    s = jnp.sin(ang)
    sin_ref[...] = jnp.where(first, -s, s)


def _rope_tables(positions):
    t = positions.size
    half = RET_KDIM // 2
    freq = ROPE_BASE ** (-jnp.arange(half, dtype=F32) / half)
    freq = jnp.tile(freq, LANES // half).reshape(1, LANES)
    tm = _pick(t, (512, 256, 128))
    return pl.pallas_call(
        _rope_kernel,
        grid=(t // tm,),
        in_specs=[pl.BlockSpec((tm, 1), lambda i: (i, 0)),
                  pl.BlockSpec((1, LANES), lambda i: (0, 0))],
        out_specs=[pl.BlockSpec((tm, LANES), lambda i: (i, 0))] * 2,
        out_shape=[jax.ShapeDtypeStruct((t, LANES), F32)] * 2,
        compiler_params=_cp(("arbitrary",)),
    )(positions.reshape(t, 1), freq)


def _ret_kernel(q_ref, k_ref, v_ref, g_ref, cos_ref, sin_ref, dm_ref, te_ref, fs_ref, cd_ref,
                nw_ref, nb_ref, o_ref, st_ref):
    L = RET_L
    n_pairs = te_ref.shape[0]
    ci = pl.program_id(2)

    @pl.when(ci == 0)
    def _():
        st_ref[...] = jnp.zeros_like(st_ref)

    cos = cos_ref[...]
    sin = sin_ref[...]
    lane = lax.broadcasted_iota(I32, (L, LANES), 1)
    rowi = lax.broadcasted_iota(I32, (L, LANES), 0)
    first = (lane % RET_KDIM) < (RET_KDIM // 2)

    def rope(x):
        sw = jnp.where(first, pltpu.roll(x, LANES - RET_KDIM // 2, 1), pltpu.roll(x, RET_KDIM // 2, 1))
        return x * cos + sw * sin

    for p in range(n_pairs):
        qr = rope(q_ref[:, p * LANES:(p + 1) * LANES])
        kr = rope(k_ref[:, p * LANES:(p + 1) * LANES]) * RET_KDIM ** -0.5
        kr16 = kr.astype(BF16)
        v16 = v_ref[:, 2 * p * HEAD_V:2 * (p + 1) * HEAD_V].astype(BF16)

        kt = (kr * te_ref[p]).T
        cd = cd_ref[p]

        def summary(frames):
            full = _dot(jnp.where(frames, kt, 0.0).astype(BF16), v16)
            return jnp.where(rowi < RET_KDIM, full[:, :HEAD_V], full[:, HEAD_V:])
        s0 = st_ref[p]
        s1 = s0 * cd + summary(lane < CHUNK)
        st_ref[p] = s1 * cd + summary(lane >= CHUNK)
        s0_16 = s0.astype(BF16)
        s1_16 = s1.astype(BF16)

        for hh in range(2):
            head = 2 * p + hh
            qh = jnp.where((lane // RET_KDIM) == hh, qr, 0.0).astype(BF16)
            scores = _dot_nt(qh, kr16) * dm_ref[head]
            o = _dot(scores.astype(BF16), v16[:, hh * HEAD_V:(hh + 1) * HEAD_V])
            inter = jnp.where(rowi < CHUNK, _dot(qh, s0_16), _dot(qh, s1_16))
            o = o + inter * fs_ref[head]
            mu = jnp.mean(o, axis=-1, keepdims=True)
            var = jnp.mean(jnp.square(o - mu), axis=-1, keepdims=True)
            sl = slice(head * HEAD_V, (head + 1) * HEAD_V)
            o = (o - mu) * lax.rsqrt(var + NORM_EPS) * nw_ref[:, sl] + nb_ref[:, sl]
            o_ref[:, sl] = (o * _silu(g_ref[:, sl])).astype(o_ref.dtype)


def _retention(proj, cos, sin, norm_w, norm_b, bsz, seq, ret_w, off_q):
    t = proj.shape[0]
    L = RET_L
    nc = seq // L
    h = ret_w // HEAD_V
    pairs = h // 2
    assert h % 2 == 0 and seq % L == 0
    kw = h * RET_KDIM
    off_k = off_q + kw
    off_v = off_k + kw
    off_g = off_v + ret_w
    assert off_v % (2 * HEAD_V) == 0 and off_g % (2 * HEAD_V) == 0
    log_gamma = jnp.log1p(-jnp.exp2(-5.0 - jnp.arange(h, dtype=F32)))
    pos = jnp.arange(CHUNK, dtype=F32)
    d_intra = jnp.exp(log_gamma[:, None, None] * jnp.abs(pos[:, None] - pos[None, :]))
    dm = jnp.zeros((h, L, L), F32)
    dm = dm.at[:, :CHUNK, :CHUNK].set(d_intra).at[:, CHUNK:, CHUNK:].set(d_intra)
    to_end = jnp.exp(log_gamma[:, None] * (CHUNK - 1 - pos))
    te = jnp.repeat(jnp.tile(to_end, (1, 2)).reshape(pairs, 2, L), RET_KDIM, axis=1)
    te = jnp.swapaxes(te, 1, 2)
    from_start = jnp.exp(log_gamma[:, None] * (pos + 1.0))
    fs = jnp.broadcast_to(jnp.tile(from_start, (1, 2))[:, :, None], (h, L, HEAD_V))
    cdec = jnp.exp(log_gamma * CHUNK)
    cd = jnp.broadcast_to(jnp.repeat(cdec.reshape(pairs, 2), RET_KDIM, axis=1)[:, :, None],
                          (pairs, 2 * RET_KDIM, HEAD_V))
    n_p = _pick(pairs, (4, 2, 1))
    wq = n_p * LANES
    wv = n_p * 2 * HEAD_V
    assert off_q % wq == 0 and off_k % wq == 0 and off_v % wv == 0 and off_g % wv == 0
    row = lambda b, p, c: b * nc + c
    tab = lambda n0, r: pl.BlockSpec((n0, r, LANES), lambda b, p, c: (p, 0, 0))
    return pl.pallas_call(
        _ret_kernel,
        grid=(bsz, pairs // n_p, nc),
        in_specs=[pl.BlockSpec((L, wq), lambda b, p, c: (row(b, p, c), off_q // wq + p)),
                  pl.BlockSpec((L, wq), lambda b, p, c: (row(b, p, c), off_k // wq + p)),
                  pl.BlockSpec((L, wv), lambda b, p, c: (row(b, p, c), off_v // wv + p)),
                  pl.BlockSpec((L, wv), lambda b, p, c: (row(b, p, c), off_g // wv + p)),
                  pl.BlockSpec((L, LANES), lambda b, p, c: (row(b, p, c), 0)),
                  pl.BlockSpec((L, LANES), lambda b, p, c: (row(b, p, c), 0)),
                  tab(2 * n_p, L),
                  tab(n_p, L),
                  tab(2 * n_p, L),
                  tab(n_p, 2 * RET_KDIM),
                  pl.BlockSpec((1, wv), lambda b, p, c: (0, p)),
                  pl.BlockSpec((1, wv), lambda b, p, c: (0, p))],
        out_specs=pl.BlockSpec((L, wv), lambda b, p, c: (row(b, p, c), p)),
        out_shape=jax.ShapeDtypeStruct((t, ret_w), BF16),
        scratch_shapes=[pltpu.VMEM((n_p, 2 * RET_KDIM, HEAD_V), F32)],
        compiler_params=_cp(("arbitrary", "arbitrary", "arbitrary")),
    )(proj, proj, proj, proj, cos, sin, dm, te, fs, cd,
      norm_w.reshape(1, ret_w), norm_b.reshape(1, ret_w))


def _layer_norm_rows(v, g, b):
    mu = jnp.mean(v, axis=-1, keepdims=True)
    var = jnp.mean(jnp.square(v - mu), axis=-1, keepdims=True)
    return (v - mu) * lax.rsqrt(var + NORM_EPS) * g + b


def _route_rows(p):
    rows = [p[e:e + 1, :] for e in range(N_EXPERTS)]
    scores = []
    for gi in range(N_EXPERT_GROUPS):
        a, b, c, d = rows[gi * EXPERTS_PER_GROUP:(gi + 1) * EXPERTS_PER_GROUP]
        hi1, lo1 = jnp.maximum(a, b), jnp.minimum(a, b)
        hi2, lo2 = jnp.maximum(c, d), jnp.minimum(c, d)
        scores.append(jnp.maximum(hi1, hi2) + jnp.maximum(jnp.minimum(hi1, hi2), jnp.maximum(lo1, lo2)))
    best = scores[0]
    sel = jnp.zeros_like(best, dtype=I32)
    for gi in range(1, N_EXPERT_GROUPS):
        better = scores[gi] > best
        best = jnp.where(better, scores[gi], best)
        sel = jnp.where(better, gi, sel)
    cand = []
    for e in range(EXPERTS_PER_GROUP):
        v = rows[e]
        for gi in range(1, N_EXPERT_GROUPS):
            v = jnp.where(sel == gi, rows[gi * EXPERTS_PER_GROUP + e], v)
        cand.append(v)

    def argbest(vals):
        bv = vals[0]
        bi = jnp.zeros_like(sel)
        for e in range(1, len(vals)):
            better = vals[e] > bv
            bv = jnp.where(better, vals[e], bv)
            bi = jnp.where(better, e, bi)
        return bv, bi
    w0, i0 = argbest(cand)
    w1, i1 = argbest([jnp.where(i0 == e, -1.0, cand[e]) for e in range(EXPERTS_PER_GROUP)])
    tot = w0 + w1
    base = sel * EXPERTS_PER_GROUP
    return base + i0, base + i1, w0 / tot, w1 / tot


def _row_to_vmem(hbm, row, buf, r, sem):
    d = buf.shape[1]
    return pltpu.make_async_copy(hbm.at[pl.ds(pl.multiple_of(row * d, d), d)], buf.at[r], sem)


def _row_to_hbm(buf, r, hbm, row, sem):
    d = buf.shape[1]
    return pltpu.make_async_copy(buf.at[r], hbm.at[pl.ds(pl.multiple_of(row * d, d), d)], sem)


def _ln_kernel(*refs, alpha, pair_rows, emit_h, h_dtype, with_router):
    it = iter(refs)
    if pair_rows:
        slot_ref = next(it)
    x_ref, y_ref = next(it), next(it)
    if pair_rows:
        w_ref = next(it)
    gate_ref, g_ref, b_ref = next(it), next(it), next(it)
    if emit_h:
        sc_ref, sh_ref = next(it), next(it)
    if with_router:
        rwh_ref, rwl_ref, rb_ref = next(it), next(it), next(it)
    xo_ref = next(it)
    if emit_h:
        h_ref = next(it)
    if with_router:
        idx_ref, wt_ref, cnt_ref = next(it), next(it), next(it)
        hbufs, hsem = (next(it), next(it)), next(it)
    if pair_rows:
        ybufs, ysem = ((next(it), next(it)), (next(it), next(it))), next(it)
    tm = x_ref.shape[0]
    i = pl.program_id(0)
    n = pl.num_programs(0)

    def finish(y):
        xn = _layer_norm_rows(alpha * x_ref[...] + (1.0 + gate_ref[...]) * y, g_ref[...], b_ref[...])
        xo_ref[...] = xn
        if emit_h:
            h = xn * (1.0 + sc_ref[...]) + sh_ref[...]
            if not with_router:
                h_ref[...] = h.astype(h_dtype)
            return h

    if pair_rows:
        t_all = n * tm

        def row_copy(slot_row, s, k, r):
            return pltpu.make_async_copy(y_ref.at[pl.ds(slot_row, 1), :], ybufs[s][k].at[pl.ds(r, 1), :],
                                         ysem.at[s])

        def fetch(tile, s, r):
            row_copy(slot_ref[tile * tm + r], s, 0, r).start()
            row_copy(slot_ref[t_all + tile * tm + r], s, 1, r).start()

        def wait_tile(s):
            for r in range(tm):
                row_copy(0, s, 0, r).wait()
                row_copy(0, s, 1, r).wait()

        @pl.when(i == 0)
        def _():
            def first(r, c):
                fetch(0, 0, r)
                return c
            lax.fori_loop(0, tm, first, 0, unroll=8)

        def step(s):
            wait_tile(s)
            nxt = jnp.minimum(i + 1, n - 1)
            for r in range(tm):
                fetch(nxt, 1 - s, r)
            finish(ybufs[s][0][...] * w_ref[:, 0:1] + ybufs[s][1][...] * w_ref[:, 1:2])

            @pl.when(i == n - 1)
            def _():
                wait_tile(1 - s)
        for parity in range(2):
            pl.when(i % 2 == parity)(functools.partial(step, parity))
        return

    h = finish(y_ref[...])
    if with_router:
        h_hi = h.astype(BF16)
        h_lo = (h - h_hi.astype(F32)).astype(BF16)
        rwh = rwh_ref[...]
        logits = (_dot_nt(rwh, h_hi) + _dot_nt(rwh, h_lo) + _dot_nt(rwl_ref[...], h_hi)
                  + rb_ref[...])
        e = jnp.exp(logits - jnp.max(logits, axis=0, keepdims=True))
        p = e / jnp.sum(e, axis=0, keepdims=True)
        i0, i1, w0, w1 = _route_rows(p)

        @pl.when(i == 0)
        def _():
            cnt_ref[...] = jnp.zeros_like(cnt_ref)
        erow = lax.broadcasted_iota(I32, logits.shape, 0)
        oh0 = (erow == i0).astype(F32)
        oh1 = (erow == i1).astype(F32)
        both = oh0 + oh1
        tr = lax.broadcasted_iota(I32, (tm, tm), 0)
        tc = lax.broadcasted_iota(I32, (tm, tm), 1)
        before = cnt_ref[:, 0:1] + _dot(both.astype(BF16), (tr < tc).astype(BF16))
        r0 = jnp.sum(oh0 * before, axis=0, keepdims=True).astype(I32)
        r1 = jnp.sum(oh1 * before, axis=0, keepdims=True).astype(I32)
        cnt_ref[...] = cnt_ref[...] + jnp.sum(both, axis=1, keepdims=True)
        zi = jnp.zeros((4, tm), I32)
        idx_ref[...] = jnp.concatenate([i0, i1, r0, r1, zi], axis=0)
        wt_ref[...] = jnp.concatenate([w0, w1, jnp.zeros((6, tm), F32)], axis=0)

        def wait_sent(s):
            for r in range(tm):
                _row_to_hbm(hbufs[s], r, h_ref, 0, hsem.at[s]).wait()

        def send(s):
            @pl.when(i >= 2)
            def _():
                wait_sent(s)
            hbufs[s][...] = h
            for r in range(tm):
                _row_to_hbm(hbufs[s], r, h_ref, i * tm + r, hsem.at[s]).start()

            @pl.when(i == n - 1)
            def _():
                wait_sent(s)
                wait_sent(1 - s)
        for parity in range(2):
            pl.when(i % 2 == parity)(functools.partial(send, parity))


def _post_norm(x2, y, gate, ln_g, ln_b, alpha, seq, *, pair_slots=None, pair_w=None, scale=None,
               shift=None, h_dtype=BF16, router=None):
    t, d = x2.shape
    tm = _pick(seq, (128, 64))
    emit_h = scale is not None
    with_router = router is not None
    pair_rows = pair_slots is not None
    assert t // tm >= 2
    vec = pl.BlockSpec((None, 1, d), lambda i, *_: (i * tm // seq, 0, 0))
    par = pl.BlockSpec((1, d), lambda i, *_: (0, 0))
    rows = pl.BlockSpec((tm, d), lambda i, *_: (i, 0))
    hbm = pl.BlockSpec(memory_space=pl.ANY)
    in_specs = [rows, hbm if pair_rows else rows]
    args = [x2, y]
    if pair_rows:
        in_specs.append(pl.BlockSpec((tm, TOP_K), lambda i, *_: (i, 0)))
        args.append(pair_w)
    in_specs += [vec, par, par]
    args += [gate, ln_g.reshape(1, d), ln_b.reshape(1, d)]
    out_specs = [rows]
    out_shape = [jax.ShapeDtypeStruct((t, d), F32)]
    scratch = []
    if emit_h:
        in_specs += [vec, vec]
        args += [scale, shift]
        if with_router:
            out_specs.append(hbm)
            out_shape.append(jax.ShapeDtypeStruct((t * d,), h_dtype))
        else:
            out_specs.append(rows)
            out_shape.append(jax.ShapeDtypeStruct((t, d), h_dtype))
    if with_router:
        rwh, rwl, rb = router
        e = rwh.shape[0]
        in_specs += [pl.BlockSpec((e, d), lambda i, *_: (0, 0)), pl.BlockSpec((e, d), lambda i, *_: (0, 0)),
                     pl.BlockSpec((e, 1), lambda i, *_: (0, 0))]
        args += [rwh, rwl, rb]
        out_specs += [pl.BlockSpec((8, tm), lambda i, *_: (0, i))] * 2
        out_shape += [jax.ShapeDtypeStruct((8, t), I32), jax.ShapeDtypeStruct((8, t), F32)]
        scratch += [pltpu.VMEM((N_EXPERTS, LANES), F32), pltpu.VMEM((tm, d), h_dtype),
                    pltpu.VMEM((tm, d), h_dtype), pltpu.SemaphoreType.DMA((2,))]
    if pair_rows:
        scratch += [pltpu.VMEM((tm, d), F32)] * 4 + [pltpu.SemaphoreType.DMA((2,))]
        args = [pair_slots] + args
    grid_spec = pltpu.PrefetchScalarGridSpec(
        num_scalar_prefetch=1 if pair_rows else 0, grid=(t // tm,),
        in_specs=in_specs, out_specs=out_specs, scratch_shapes=scratch)
    return pl.pallas_call(
        functools.partial(_ln_kernel, alpha=alpha, pair_rows=pair_rows, emit_h=emit_h,
                          h_dtype=h_dtype, with_router=with_router),
        grid_spec=grid_spec, out_shape=out_shape,
        compiler_params=_cp(("arbitrary",)),
    )(*args)


def _moe_kernel(src_ref, exp_ref, h_hbm, wg_ref, wu_ref, wd_ref, y_ref, xbuf0, xbuf1, gsem):
    i = pl.program_id(0)
    nb = pl.num_programs(0)
    R = MOE_BLOCK
    xbufs = (xbuf0, xbuf1)

    def gather_row(block, s, r):
        _row_to_vmem(h_hbm, src_ref[block * R + r], xbufs[s], r, gsem.at[s]).start()

    def wait_gather(s):
        for r in range(R):
            _row_to_vmem(h_hbm, 0, xbufs[s], r, gsem.at[s]).wait()

    @pl.when(i == 0)
    def _():
        def first(r, c):
            gather_row(0, 0, r)
            return c
        lax.fori_loop(0, R, first, 0, unroll=8)

    def step(slot):
        other = 1 - slot
        wait_gather(slot)
        nxt = jnp.minimum(i + 1, nb - 1)
        for r in range(R):
            gather_row(nxt, other, r)
        x = xbufs[slot][...].astype(BF16)
        hid = _silu(_dot(x, wg_ref[...])) * _dot(x, wu_ref[...])
        y_ref[...] = _dot(hid.astype(BF16), wd_ref[...])

        @pl.when(i == nb - 1)
        def _():
            wait_gather(other)

    for parity in range(2):
        pl.when(i % 2 == parity)(functools.partial(step, parity))


def _moe_blocks(h_flat, slot_src, block_exp, wg, wu, wd):
    d = wg.shape[1]
    f = wg.shape[2]
    n_blocks = block_exp.shape[0]
    R = MOE_BLOCK
    grid_spec = pltpu.PrefetchScalarGridSpec(
        num_scalar_prefetch=2,
        grid=(n_blocks,),
        in_specs=[pl.BlockSpec(memory_space=pl.ANY),
                  pl.BlockSpec((None, d, f), lambda i, s, e: (e[i], 0, 0)),
                  pl.BlockSpec((None, d, f), lambda i, s, e: (e[i], 0, 0)),
                  pl.BlockSpec((None, f, d), lambda i, s, e: (e[i], 0, 0))],
        out_specs=pl.BlockSpec((R, d), lambda i, s, e: (i, 0)),
        scratch_shapes=[pltpu.VMEM((R, d), F32), pltpu.VMEM((R, d), F32),
                        pltpu.SemaphoreType.DMA((2,))],
    )
    return pl.pallas_call(
        _moe_kernel,
        grid_spec=grid_spec,
        out_shape=jax.ShapeDtypeStruct((n_blocks * R, d), F32),
        compiler_params=_cp(("arbitrary",)),
    )(slot_src, block_exp, h_flat, wg, wu, wd)


def _dispatch_plan(idx, t):
    n_assign = t * TOP_K
    e01 = idx[:TOP_K]
    rank01 = idx[TOP_K:2 * TOP_K]
    experts = jnp.arange(N_EXPERTS, dtype=I32)
    onehot = (e01[:, :, None] == experts).astype(I32)
    counts = jnp.sum(onehot, axis=(0, 1))
    padded = (counts + MOE_BLOCK - 1) // MOE_BLOCK * MOE_BLOCK
    pad_end = jnp.cumsum(padded)
    pad_start = pad_end - padded
    dest = jnp.sum(onehot * pad_start, axis=-1) + rank01
    n_blocks = -(-n_assign // MOE_BLOCK) + N_EXPERTS
    n_slots = n_blocks * MOE_BLOCK
    block_start = jnp.arange(n_blocks, dtype=I32) * MOE_BLOCK
    block_exp = jnp.minimum(jnp.sum(block_start[:, None] >= pad_end[None, :], -1), N_EXPERTS - 1).astype(I32)
    tok = jnp.arange(t, dtype=I32)
    slot_src = jnp.zeros((n_slots,), I32).at[dest.reshape(-1)].set(jnp.concatenate([tok, tok]))
    return slot_src, block_exp, dest.reshape(-1)


def kernel(x, c, positions, w_in, conv_w, conv_b, dt_bias, a_log, d_skip, ssd_norm_w, hgrn_gamma,
           hgrn_norm_w, ret_norm_w, ret_norm_b, w_out, ada_down, ada_up, ada_b, ln_g, ln_b,
           router_w, router_b, w_gate, w_up, w_down):
    bsz, seq, d = x.shape
    depth = w_in.shape[0]
    t = bsz * seq
    ssd_w = d // 2
    ssd_heads = ssd_w // SSD_HEAD_DIM
    conv_dim = ssd_w + 2 * SSD_GROUPS * SSD_STATE
    hg_w = d // 4
    ret_w = d // 4
    ret_kw = (ret_w // HEAD_V) * RET_KDIM
    alpha = (2.0 * depth) ** 0.25
    off_dt = ssd_w + conv_dim
    off_hq = off_dt
    off_rq = off_hq + 4 * hg_w

    p = jax.nn.softmax(hgrn_gamma.astype(F32), axis=0)
    lower_bounds = jnp.cumsum(p, axis=0) - p[0]
    mod = _modulation(c, ada_down, ada_up, ada_b)
    cos, sin = _rope_tables(positions)

    rwt = router_w.T
    rw_hi = rwt.astype(BF16)
    rw_lo = (rwt - rw_hi.astype(F32)).astype(BF16)
    rb = router_b.reshape(N_EXPERTS, 1).astype(F32)

    x2 = x.reshape(t, d)
    h = _modulate(x2, mod[0, 1], mod[0, 0], seq)
    for l in range(depth):
        shift1, scale1, gate1, shift2, scale2, gate2 = [mod[l, i] for i in range(N_MOD)]
        w_main = jnp.concatenate([w_in[l][:, :off_dt], w_in[l][:, off_dt + ssd_heads:]], axis=1).astype(BF16)
        w_dt = jnp.pad(w_in[l][:, off_dt:off_dt + ssd_heads], ((0, 0), (0, LANES - ssd_heads))).astype(BF16)
        proj = _matmul(h, w_main, F32)
        dt_raw = _matmul(h, w_dt, F32)
        y_ssd = _ssd(proj, dt_raw, conv_w[l], conv_b[l], dt_bias[l], a_log[l], d_skip[l], ssd_norm_w[l],
                     bsz, seq, ssd_w)
        y_hg = _hgrn(proj, lower_bounds[l], hgrn_norm_w[l], bsz, seq, hg_w, off_hq)
        y_ret = _retention(proj, cos, sin, ret_norm_w[l], ret_norm_b[l], bsz, seq, ret_w, off_rq)
        mixed = _out_proj(y_ssd, y_hg, y_ret, w_out[l].astype(BF16))
        x2, h2, idx, wts = _post_norm(x2, mixed, gate1, ln_g[l, 0], ln_b[l, 0], alpha, seq,
                                      scale=scale2, shift=shift2, h_dtype=F32, router=(rw_hi, rw_lo, rb))
        slot_src, block_exp, pair_slots = _dispatch_plan(idx, t)
        y2 = _moe_blocks(h2, slot_src, block_exp,
                         w_gate[l].astype(BF16), w_up[l].astype(BF16), w_down[l].astype(BF16))
        pair_w = wts[:TOP_K].T
        if l + 1 < depth:
            x2, h = _post_norm(x2, y2, gate2, ln_g[l, 1], ln_b[l, 1], alpha, seq, pair_slots=pair_slots,
                               pair_w=pair_w, scale=mod[l + 1, 1], shift=mod[l + 1, 0])
        else:
            (x2,) = _post_norm(x2, y2, gate2, ln_g[l, 1], ln_b[l, 1], alpha, seq, pair_slots=pair_slots,
                               pair_w=pair_w)
    return x2.reshape(bsz, seq, d)
```

```python
import functools
import math

import jax
import jax.numpy as jnp
import numpy as np
from jax import lax
from jax.experimental import pallas as pl
from jax.experimental.pallas import tpu as pltpu

F32 = jnp.float32
BF16 = jnp.bfloat16
I32 = jnp.int32

CHUNK = 64
TINY = 1e-30
NORM_EPS = 1e-5
ROPE_BASE = 10000.0
LANES = 128
SSD_HEAD_DIM = 64
SSD_GROUPS = 4
SSD_STATE = 128
SSD_CONV = 4
HEAD_V = 128
RET_KDIM = 64
N_EXPERTS = 16
N_EXPERT_GROUPS = 4
EXPERTS_PER_GROUP = N_EXPERTS // N_EXPERT_GROUPS
TOP_K = 2
MOE_BLOCK = 256
N_MOD = 6
VMEM_LIMIT = 56 * 1024 * 1024

SSD_L = 128
HG_L = 64
HG_SUB = 16
HG_SAFE_DECAY = 60.0
RET_L = 2 * CHUNK


def _cp(sem):
    return pltpu.CompilerParams(dimension_semantics=sem, vmem_limit_bytes=VMEM_LIMIT)


def _pick(n, cands):
    for c in cands:
        if n % c == 0:
            return c
    return n


def _silu(x):
    return x * jax.nn.sigmoid(x)


def _split3(v):
    hi = v.astype(BF16)
    r1 = v - hi.astype(F32)
    mid = r1.astype(BF16)
    lo = (r1 - mid.astype(F32)).astype(BF16)
    return hi, mid, lo


def _dot(a, b):
    return jnp.dot(a, b, preferred_element_type=F32)


def _dot_nt(a, b):
    return lax.dot_general(a, b, (((1,), (1,)), ((), ())), preferred_element_type=F32)


def _sel_right(v, m01):
    hi, mid, lo = _split3(v)
    return _dot(hi, m01) + _dot(mid, m01) + _dot(lo, m01)


def _sel_left(m01, v):
    hi, mid, lo = _split3(v)
    return _dot(m01, hi) + _dot(m01, mid) + _dot(m01, lo)


def _tri(n):
    r = lax.broadcasted_iota(I32, (n, n), 0)
    c = lax.broadcasted_iota(I32, (n, n), 1)
    return r >= c


def _mod_kernel(c_ref, down_ref, up_ref, b_ref, o_ref):
    c = c_ref[...]
    t = jnp.dot(_silu(c), down_ref[...], preferred_element_type=F32,
                precision=lax.Precision.HIGHEST)
    o_ref[...] = jnp.dot(t, up_ref[...], preferred_element_type=F32,
                         precision=lax.Precision.HIGHEST) + b_ref[...]


def _modulation(c, ada_down, ada_up, ada_b):
    depth, d, rank = ada_down.shape
    bsz = c.shape[0]
    rows = 8
    cp = jnp.zeros((rows, d), F32).at[:bsz].set(c)
    out = pl.pallas_call(
        _mod_kernel,
        grid=(depth, N_MOD),
        in_specs=[pl.BlockSpec((rows, d), lambda l, j: (0, 0)),
                  pl.BlockSpec((None, d, rank), lambda l, j: (l, 0, 0)),
                  pl.BlockSpec((None, rank, d), lambda l, j: (l, 0, j)),
                  pl.BlockSpec((None, 1, d), lambda l, j: (l, 0, j))],
        out_specs=pl.BlockSpec((None, None, rows, d), lambda l, j: (l, j, 0, 0)),
        out_shape=jax.ShapeDtypeStruct((depth, N_MOD, rows, d), F32),
        compiler_params=_cp(("arbitrary", "arbitrary")),
    )(cp, ada_down, ada_up, ada_b.reshape(depth, 1, N_MOD * d))
    return out[:, :, :bsz, None, :]


def _modulate_kernel(x_ref, sc_ref, sh_ref, h_ref):
    h_ref[...] = (x_ref[...] * (1.0 + sc_ref[...]) + sh_ref[...]).astype(h_ref.dtype)


def _modulate(x2, scale, shift, seq):
    t, d = x2.shape
    tm = _pick(seq, (512, 256, 128, 64))
    vec = pl.BlockSpec((None, 1, d), lambda i: (i * tm // seq, 0, 0))
    return pl.pallas_call(
        _modulate_kernel,
        grid=(t // tm,),
        in_specs=[pl.BlockSpec((tm, d), lambda i: (i, 0)), vec, vec],
        out_specs=pl.BlockSpec((tm, d), lambda i: (i, 0)),
        out_shape=jax.ShapeDtypeStruct((t, d), BF16),
        compiler_params=_cp(("arbitrary",)),
    )(x2, scale, shift)


def _mm_kernel(a_ref, b_ref, o_ref):
    o_ref[...] = _dot(a_ref[...], b_ref[...]).astype(o_ref.dtype)


def _matmul(a, b, out_dtype):
    m, k = a.shape
    n = b.shape[1]
    tm = _pick(m, (1024, 512, 256, 128))
    tn = _pick(n, (1024, 768, 512, 384, 256, 128))
    return pl.pallas_call(
        _mm_kernel,
        grid=(m // tm, n // tn),
        in_specs=[pl.BlockSpec((tm, k), lambda i, j: (i, 0)),
                  pl.BlockSpec((k, tn), lambda i, j: (0, j))],
        out_specs=pl.BlockSpec((tm, tn), lambda i, j: (i, j)),
        out_shape=jax.ShapeDtypeStruct((m, n), out_dtype),
        compiler_params=_cp(("arbitrary", "arbitrary")),
    )(a, b)


def _mm3_kernel(a1_ref, a2_ref, a3_ref, b1_ref, b2_ref, b3_ref, o_ref):
    o_ref[...] = (_dot(a1_ref[...], b1_ref[...]) + _dot(a2_ref[...], b2_ref[...])
                  + _dot(a3_ref[...], b3_ref[...]))


def _out_proj(y_ssd, y_hg, y_ret, w_out):
    m = y_ssd.shape[0]
    d = w_out.shape[1]
    k1, k2, k3 = y_ssd.shape[1], y_hg.shape[1], y_ret.shape[1]
    tm = _pick(m, (1024, 512, 256, 128))
    tn = _pick(d, (1024, 512, 256, 128))
    return pl.pallas_call(
        _mm3_kernel,
        grid=(m // tm, d // tn),
        in_specs=[pl.BlockSpec((tm, k1), lambda i, j: (i, 0)),
                  pl.BlockSpec((tm, k2), lambda i, j: (i, 0)),
                  pl.BlockSpec((tm, k3), lambda i, j: (i, 0)),
                  pl.BlockSpec((k1, tn), lambda i, j: (0, j)),
                  pl.BlockSpec((k2, tn), lambda i, j: (k1 // k2, j)),
                  pl.BlockSpec((k3, tn), lambda i, j: ((k1 + k2) // k3, j))],
        out_specs=pl.BlockSpec((tm, tn), lambda i, j: (i, j)),
        out_shape=jax.ShapeDtypeStruct((m, d), F32),
        compiler_params=_cp(("arbitrary", "arbitrary")),
    )(y_ssd, y_hg, y_ret, w_out, w_out, w_out)


def _ssd_kernel(z_ref, xs_ref, b_ref, c_ref, dt_ref,
                wx_ref, wb_ref, wc_ref, bx_ref, bb_ref, bc_ref,
                dtb_ref, alog_ref, dskip_ref, nw_ref, e_ref, sel_ref,
                o_ref, st_ref, px_ref, pb_ref, pc_ref, *, heads_per_group):
    L = SSD_L
    n = SSD_STATE
    groups = sel_ref.shape[0]
    gw = o_ref.shape[1] // groups
    ci = pl.program_id(1)

    @pl.when(ci == 0)
    def _():
        st_ref[...] = jnp.zeros_like(st_ref)
        px_ref[0:8, :] = jnp.zeros((8, px_ref.shape[1]), F32)
        pb_ref[0:8, :] = jnp.zeros((8, pb_ref.shape[1]), F32)
        pc_ref[0:8, :] = jnp.zeros((8, pc_ref.shape[1]), F32)

    def conv_silu(raw_ref, pad_ref, w_ref, bias_ref):
        pad_ref[8:8 + L, :] = raw_ref[...]
        acc = bias_ref[...]
        for k in range(SSD_CONV):
            acc = acc + w_ref[k:k + 1, :] * pad_ref[pl.ds(8 - (SSD_CONV - 1) + k, L), :]
        pad_ref[0:8, :] = pad_ref[L:L + 8, :]
        return _silu(acc)

    xs_all = conv_silu(xs_ref, px_ref, wx_ref, bx_ref)
    bm_all = conv_silu(b_ref, pb_ref, wb_ref, bb_ref)
    cm_all = conv_silu(c_ref, pc_ref, wc_ref, bc_ref)

    dt = jax.nn.softplus(dt_ref[...] + dtb_ref[...])
    dta = dt * (-jnp.exp(alog_ref[...]))
    tri = _tri(L)
    cum = _sel_left(tri.astype(BF16), dta)
    e01 = e_ref[...]
    dt_full_all = _sel_right(dt, e01)
    cum_full_all = _sel_right(cum, e01)
    lane = lax.broadcasted_iota(I32, (L, LANES), 1)
    lo_half = lane < SSD_HEAD_DIM

    for g in range(groups):
        sl = slice(g * gw, (g + 1) * gw)
        xs = xs_all[:, sl]
        bm = bm_all[:, g * n:(g + 1) * n]
        cm = cm_all[:, g * n:(g + 1) * n]
        cum_g = _sel_right(cum, sel_ref[g])
        cum_gt = cum_g.T
        cum_full = cum_full_all[:, sl]
        last = cum_full[L - 1:L, :]
        to_end = jnp.exp(last - cum_full)
        from_start = jnp.exp(cum_full)
        xdt = xs * dt_full_all[:, sl]

        bm16 = bm.astype(BF16)
        cm16 = cm.astype(BF16)
        cb = _dot_nt(cm16, bm16)
        ys = []
        for j in range(heads_per_group // 2):
            atts = []
            for i in (2 * j, 2 * j + 1):
                dec = jnp.where(tri, jnp.exp(jnp.minimum(cum_g[:, i:i + 1] - cum_gt[i:i + 1, :], 0.0)), 0.0)
                atts.append((cb * dec).astype(BF16))
            xp = xdt[:, j * LANES:(j + 1) * LANES]
            stack = jnp.concatenate([jnp.where(lo_half, xp, 0.0), jnp.where(lo_half, 0.0, xp)], axis=0)
            ys.append(_dot(jnp.concatenate(atts, axis=1), stack.astype(BF16)))
        y = jnp.concatenate(ys, axis=1) if len(ys) > 1 else ys[0]

        st = st_ref[g]
        y = y + _dot(cm16, st.astype(BF16)) * from_start
        st_ref[g] = st * jnp.exp(last) + _dot(bm.T.astype(BF16), (xdt * to_end).astype(BF16))

        y = y + xs * dskip_ref[:, sl]
        y = y * _silu(z_ref[:, sl])
        y = y * lax.rsqrt(jnp.mean(jnp.square(y), axis=-1, keepdims=True) + NORM_EPS) * nw_ref[:, sl]
        o_ref[:, sl] = y.astype(o_ref.dtype)


def _ssd(proj, dt_raw, conv_w, conv_b, dt_bias, a_log, d_skip, norm_w, bsz, seq, ssd_w):
    t = proj.shape[0]
    L = SSD_L
    nc = seq // L
    g = SSD_GROUPS
    gw = ssd_w // g
    n = SSD_STATE
    gn = g * n
    heads = ssd_w // SSD_HEAD_DIM
    r = heads // g
    assert r % 2 == 0 and gw % LANES == 0 and seq % L == 0 and heads <= LANES and ssd_w % gn == 0
    off_b = 2 * ssd_w
    off_c = off_b + gn
    hpad = LANES - heads
    dtb = jnp.pad(dt_bias, (0, hpad)).reshape(1, LANES)
    alog = jnp.pad(a_log, (0, hpad)).reshape(1, LANES)
    dskip = jnp.repeat(d_skip, SSD_HEAD_DIM).reshape(1, ssd_w)
    hid = jnp.arange(LANES)
    e01 = (hid[:, None] == (jnp.arange(ssd_w) // SSD_HEAD_DIM)[None, :]).astype(BF16)
    sel = (hid[None, :, None] == (jnp.arange(g)[:, None, None] * r + hid[None, None, :])
           ) & (hid[None, None, :] < r)
    sel = sel.astype(BF16)
    row = lambda b, c: b * nc + c
    cw = conv_w
    cb2 = conv_b.reshape(1, -1)
    fixed = lambda shape, col: pl.BlockSpec(shape, lambda b, c: (0, col))
    in_specs = [
        pl.BlockSpec((L, ssd_w), lambda b, c: (row(b, c), 0)),
        pl.BlockSpec((L, ssd_w), lambda b, c: (row(b, c), 1)),
        pl.BlockSpec((L, gn), lambda b, c: (row(b, c), off_b // gn)),
        pl.BlockSpec((L, gn), lambda b, c: (row(b, c), off_c // gn)),
        pl.BlockSpec((L, LANES), lambda b, c: (row(b, c), 0)),
        fixed((SSD_CONV, ssd_w), 0),
        fixed((SSD_CONV, gn), ssd_w // gn),
        fixed((SSD_CONV, gn), ssd_w // gn + 1),
        fixed((1, ssd_w), 0),
        fixed((1, gn), ssd_w // gn),
        fixed((1, gn), ssd_w // gn + 1),
        fixed((1, LANES), 0),
        fixed((1, LANES), 0),
        fixed((1, ssd_w), 0),
        fixed((1, ssd_w), 0),
        fixed((LANES, ssd_w), 0),
        pl.BlockSpec((g, LANES, LANES), lambda b, c: (0, 0, 0)),
    ]
    return pl.pallas_call(
        functools.partial(_ssd_kernel, heads_per_group=r),
        grid=(bsz, nc),
        in_specs=in_specs,
        out_specs=pl.BlockSpec((L, ssd_w), lambda b, c: (row(b, c), 0)),
        out_shape=jax.ShapeDtypeStruct((t, ssd_w), BF16),
        scratch_shapes=[pltpu.VMEM((g, n, gw), F32),
                        pltpu.VMEM((L + 8, ssd_w), F32),
                        pltpu.VMEM((L + 8, gn), F32),
                        pltpu.VMEM((L + 8, gn), F32)],
        compiler_params=_cp(("arbitrary", "arbitrary")),
    )(proj, proj, proj, proj, dt_raw, cw, cw, cw, cb2, cb2, cb2,
      dtb, alog, dskip, norm_w.reshape(1, ssd_w), e01, sel)


def _hgrn_kernel(q_ref, f_ref, v_ref, g_ref, lb_ref, nw_ref, o_ref, st_ref):
    L = HG_L
    SB = HG_SUB
    nsb = L // SB
    rows = q_ref.shape[0]
    ci = pl.program_id(2)

    @pl.when(ci == 0)
    def _():
        st_ref[...] = jnp.zeros_like(st_ref)

    lb = lb_ref[...]
    nw = nw_ref[...]
    tri = _tri(L).astype(BF16)
    ones = jnp.ones((LANES, LANES), BF16)
    zpad = jnp.zeros((LANES - L, LANES), F32)
    lane = lax.broadcasted_iota(I32, (SB, LANES), 1)
    rowi = lax.broadcasted_iota(I32, (SB, LANES), 0)

    def log_decay(fr):
        return jnp.log(jnp.maximum(lb + (1.0 - lb) * jax.nn.sigmoid(fr), TINY))

    sub_decay = jnp.sum((-log_decay(f_ref[...])).reshape(rows // SB, SB, LANES), axis=1)
    small_decay = jnp.max(sub_decay) < HG_SAFE_DECAY

    def scores_factored(q, key, bcum):
        att_rows = []
        for i in range(nsb):
            lo, hi = i * SB, (i + 1) * SB
            edge = bcum[lo - 1:lo] if i > 0 else jnp.zeros((1, LANES), F32)
            qs = (q[lo:hi] * jnp.exp(bcum[lo:hi] - edge)).astype(BF16)
            ks = key[:hi] * jnp.exp(edge - bcum[:hi])
            ks = jnp.concatenate([ks, jnp.zeros((LANES - hi, LANES), F32)], axis=0).astype(BF16)
            att_rows.append(jnp.where(rowi + lo >= lane, _dot_nt(qs, ks), 0.0))
        return jnp.concatenate(att_rows, axis=0)

    def scores_exact(q, key, bcum):
        pieces = []
        for i in range(nsb):
            qi = q[i * SB:(i + 1) * SB]
            bi = bcum[i * SB:(i + 1) * SB]
            for j in range(SB):
                r = i * SB + j
                pieces.append((qi * key[r:r + 1] * jnp.exp(jnp.minimum(bi - bcum[r:r + 1], 0.0))).astype(BF16))
        sums = _dot(jnp.concatenate(pieces, axis=0), ones)

        att_rows = []
        for i in range(nsb):
            diag = jnp.zeros((SB, LANES), F32)
            for j in range(SB):
                r = i * SB + j
                diag = jnp.where(lane == r, sums[r * SB:(r + 1) * SB], diag)
            att_i = jnp.where(rowi + i * SB >= lane, diag, 0.0)
            if i > 0:
                edge = bcum[i * SB - 1:i * SB]
                qs = (q[i * SB:(i + 1) * SB] * jnp.exp(bcum[i * SB:(i + 1) * SB] - edge)).astype(BF16)
                ks = jnp.concatenate([key * jnp.exp(jnp.minimum(edge - bcum, 0.0)), zpad], axis=0).astype(BF16)
                att_i = jnp.where(lane < i * SB, _dot_nt(qs, ks), att_i)
            att_rows.append(att_i)
        return jnp.concatenate(att_rows, axis=0)

    def chunk(c, scores):
        r0 = c * L
        q = q_ref[r0:r0 + L, :]
        fr = f_ref[r0:r0 + L, :]
        v = v_ref[r0:r0 + L, :]
        log_f = log_decay(fr)
        key = (1.0 - lb) * jax.nn.sigmoid(-fr)
        bcum = _sel_left(tri, log_f)
        last = bcum[L - 1:L, :]
        att = scores(q, key, bcum)

        vpad = jnp.concatenate([v, zpad], axis=0)
        st = st_ref[...]
        o = (_dot_nt((q * jnp.exp(bcum)).astype(BF16), st.astype(BF16))
             + _dot(att.astype(BF16), vpad.astype(BF16)))
        kk = jnp.concatenate([key * jnp.exp(last - bcum), zpad], axis=0)
        st_ref[...] = st * jnp.exp(last) + _dot(vpad.T.astype(BF16), kk.astype(BF16))

        o = o * lax.rsqrt(jnp.mean(jnp.square(o), axis=-1, keepdims=True) + NORM_EPS) * nw
        o_ref[r0:r0 + L, :] = (o * _silu(g_ref[r0:r0 + L, :])).astype(o_ref.dtype)

    @pl.when(small_decay)
    def _():
        for c in range(rows // L):
            chunk(c, scores_factored)

    @pl.when(jnp.logical_not(small_decay))
    def _():
        for c in range(rows // L):
            chunk(c, scores_exact)


def _hgrn(proj, lower_bound, norm_w, bsz, seq, hg_w, off_q):
    t = proj.shape[0]
    rows = _pick(seq, (8 * HG_L, 4 * HG_L, 2 * HG_L, HG_L))
    nc = seq // rows
    h = hg_w // HEAD_V
    cq = off_q // LANES
    row = lambda b, hi, c: b * nc + c
    blk = lambda k: pl.BlockSpec((rows, LANES), lambda b, hi, c: (row(b, hi, c), cq + k * h + hi))
    vec = pl.BlockSpec((1, LANES), lambda b, hi, c: (0, hi))
    return pl.pallas_call(
        _hgrn_kernel,
        grid=(bsz, h, nc),
        in_specs=[blk(0), blk(1), blk(2), blk(3), vec, vec],
        out_specs=pl.BlockSpec((rows, LANES), lambda b, hi, c: (row(b, hi, c), hi)),
        out_shape=jax.ShapeDtypeStruct((t, hg_w), BF16),
        scratch_shapes=[pltpu.VMEM((HEAD_V, LANES), F32)],
        compiler_params=_cp(("arbitrary", "arbitrary", "arbitrary")),
    )(proj, proj, proj, proj, lower_bound.reshape(1, hg_w), norm_w.reshape(1, hg_w))


def _rope_kernel(pos_ref, freq_ref, cos_ref, sin_ref):
    ang = pos_ref[...].astype(F32) * freq_ref[...]
    lane = lax.broadcasted_iota(I32, ang.shape, 1)
    first = (lane % RET_KDIM) < (RET_KDIM // 2)
    cos_ref[...] = jnp.cos(ang)
    s = jnp.sin(ang)
    sin_ref[...] = jnp.where(first, -s, s)


def _rope_tables(positions):
    t = positions.size
    half = RET_KDIM // 2
    freq = ROPE_BASE ** (-jnp.arange(half, dtype=F32) / half)
    freq = jnp.tile(freq, LANES // half).reshape(1, LANES)
    tm = _pick(t, (512, 256, 128))
    return pl.pallas_call(
        _rope_kernel,
        grid=(t // tm,),
        in_specs=[pl.BlockSpec((tm, 1), lambda i: (i, 0)),
                  pl.BlockSpec((1, LANES), lambda i: (0, 0))],
        out_specs=[pl.BlockSpec((tm, LANES), lambda i: (i, 0))] * 2,
        out_shape=[jax.ShapeDtypeStruct((t, LANES), F32)] * 2,
        compiler_params=_cp(("arbitrary",)),
    )(positions.reshape(t, 1), freq)


def _ret_kernel(q_ref, k_ref, v_ref, g_ref, cos_ref, sin_ref, dm_ref, te_ref, fs_ref, cd_ref,
                nw_ref, nb_ref, o_ref, st_ref):
    L = RET_L
    n_pairs = te_ref.shape[0]
    ci = pl.program_id(2)

    @pl.when(ci == 0)
    def _():
        st_ref[...] = jnp.zeros_like(st_ref)

    cos = cos_ref[...]
    sin = sin_ref[...]
    lane = lax.broadcasted_iota(I32, (L, LANES), 1)
    rowi = lax.broadcasted_iota(I32, (L, LANES), 0)
    first = (lane % RET_KDIM) < (RET_KDIM // 2)

    def rope(x):
        sw = jnp.where(first, pltpu.roll(x, LANES - RET_KDIM // 2, 1), pltpu.roll(x, RET_KDIM // 2, 1))
        return x * cos + sw * sin

    for p in range(n_pairs):
        qr = rope(q_ref[:, p * LANES:(p + 1) * LANES])
        kr = rope(k_ref[:, p * LANES:(p + 1) * LANES]) * RET_KDIM ** -0.5
        kr16 = kr.astype(BF16)
        v16 = v_ref[:, 2 * p * HEAD_V:2 * (p + 1) * HEAD_V].astype(BF16)

        kt = (kr * te_ref[p]).T
        cd = cd_ref[p]

        def summary(frames):
            full = _dot(jnp.where(frames, kt, 0.0).astype(BF16), v16)
            return jnp.where(rowi < RET_KDIM, full[:, :HEAD_V], full[:, HEAD_V:])
        s0 = st_ref[p]
        s1 = s0 * cd + summary(lane < CHUNK)
        st_ref[p] = s1 * cd + summary(lane >= CHUNK)
        s0_16 = s0.astype(BF16)
        s1_16 = s1.astype(BF16)

        for hh in range(2):
            head = 2 * p + hh
            qh = jnp.where((lane // RET_KDIM) == hh, qr, 0.0).astype(BF16)
            scores = _dot_nt(qh, kr16) * dm_ref[head]
            o = _dot(scores.astype(BF16), v16[:, hh * HEAD_V:(hh + 1) * HEAD_V])
            inter = jnp.where(rowi < CHUNK, _dot(qh, s0_16), _dot(qh, s1_16))
            o = o + inter * fs_ref[head]
            mu = jnp.mean(o, axis=-1, keepdims=True)
            var = jnp.mean(jnp.square(o - mu), axis=-1, keepdims=True)
            sl = slice(head * HEAD_V, (head + 1) * HEAD_V)
            o = (o - mu) * lax.rsqrt(var + NORM_EPS) * nw_ref[:, sl] + nb_ref[:, sl]
            o_ref[:, sl] = (o * _silu(g_ref[:, sl])).astype(o_ref.dtype)


def _retention(proj, cos, sin, norm_w, norm_b, bsz, seq, ret_w, off_q):
    t = proj.shape[0]
    L = RET_L
    nc = seq // L
    h = ret_w // HEAD_V
    pairs = h // 2
    assert h % 2 == 0 and seq % L == 0
    kw = h * RET_KDIM
    off_k = off_q + kw
    off_v = off_k + kw
    off_g = off_v + ret_w
    assert off_v % (2 * HEAD_V) == 0 and off_g % (2 * HEAD_V) == 0
    log_gamma = jnp.log1p(-jnp.exp2(-5.0 - jnp.arange(h, dtype=F32)))
    pos = jnp.arange(CHUNK, dtype=F32)
    d_intra = jnp.exp(log_gamma[:, None, None] * jnp.abs(pos[:, None] - pos[None, :]))
    dm = jnp.zeros((h, L, L), F32)
    dm = dm.at[:, :CHUNK, :CHUNK].set(d_intra).at[:, CHUNK:, CHUNK:].set(d_intra)
    to_end = jnp.exp(log_gamma[:, None] * (CHUNK - 1 - pos))
    te = jnp.repeat(jnp.tile(to_end, (1, 2)).reshape(pairs, 2, L), RET_KDIM, axis=1)
    te = jnp.swapaxes(te, 1, 2)
    from_start = jnp.exp(log_gamma[:, None] * (pos + 1.0))
    fs = jnp.broadcast_to(jnp.tile(from_start, (1, 2))[:, :, None], (h, L, HEAD_V))
    cdec = jnp.exp(log_gamma * CHUNK)
    cd = jnp.broadcast_to(jnp.repeat(cdec.reshape(pairs, 2), RET_KDIM, axis=1)[:, :, None],
                          (pairs, 2 * RET_KDIM, HEAD_V))
    n_p = _pick(pairs, (4, 2, 1))
    wq = n_p * LANES
    wv = n_p * 2 * HEAD_V
    assert off_q % wq == 0 and off_k % wq == 0 and off_v % wv == 0 and off_g % wv == 0
    row = lambda b, p, c: b * nc + c
    tab = lambda n0, r: pl.BlockSpec((n0, r, LANES), lambda b, p, c: (p, 0, 0))
    return pl.pallas_call(
        _ret_kernel,
        grid=(bsz, pairs // n_p, nc),
        in_specs=[pl.BlockSpec((L, wq), lambda b, p, c: (row(b, p, c), off_q // wq + p)),
                  pl.BlockSpec((L, wq), lambda b, p, c: (row(b, p, c), off_k // wq + p)),
                  pl.BlockSpec((L, wv), lambda b, p, c: (row(b, p, c), off_v // wv + p)),
                  pl.BlockSpec((L, wv), lambda b, p, c: (row(b, p, c), off_g // wv + p)),
                  pl.BlockSpec((L, LANES), lambda b, p, c: (row(b, p, c), 0)),
                  pl.BlockSpec((L, LANES), lambda b, p, c: (row(b, p, c), 0)),
                  tab(2 * n_p, L),
                  tab(n_p, L),
                  tab(2 * n_p, L),
                  tab(n_p, 2 * RET_KDIM),
                  pl.BlockSpec((1, wv), lambda b, p, c: (0, p)),
                  pl.BlockSpec((1, wv), lambda b, p, c: (0, p))],
        out_specs=pl.BlockSpec((L, wv), lambda b, p, c: (row(b, p, c), p)),
        out_shape=jax.ShapeDtypeStruct((t, ret_w), BF16),
        scratch_shapes=[pltpu.VMEM((n_p, 2 * RET_KDIM, HEAD_V), F32)],
        compiler_params=_cp(("arbitrary", "arbitrary", "arbitrary")),
    )(proj, proj, proj, proj, cos, sin, dm, te, fs, cd,
      norm_w.reshape(1, ret_w), norm_b.reshape(1, ret_w))


def _layer_norm_rows(v, g, b):
    mu = jnp.mean(v, axis=-1, keepdims=True)
    var = jnp.mean(jnp.square(v - mu), axis=-1, keepdims=True)
    return (v - mu) * lax.rsqrt(var + NORM_EPS) * g + b


def _route_rows(p):
    rows = [p[e:e + 1, :] for e in range(N_EXPERTS)]
    scores = []
    for gi in range(N_EXPERT_GROUPS):
        a, b, c, d = rows[gi * EXPERTS_PER_GROUP:(gi + 1) * EXPERTS_PER_GROUP]
        hi1, lo1 = jnp.maximum(a, b), jnp.minimum(a, b)
        hi2, lo2 = jnp.maximum(c, d), jnp.minimum(c, d)
        scores.append(jnp.maximum(hi1, hi2) + jnp.maximum(jnp.minimum(hi1, hi2), jnp.maximum(lo1, lo2)))
    best = scores[0]
    sel = jnp.zeros_like(best, dtype=I32)
    for gi in range(1, N_EXPERT_GROUPS):
        better = scores[gi] > best
        best = jnp.where(better, scores[gi], best)
        sel = jnp.where(better, gi, sel)
    cand = []
    for e in range(EXPERTS_PER_GROUP):
        v = rows[e]
        for gi in range(1, N_EXPERT_GROUPS):
            v = jnp.where(sel == gi, rows[gi * EXPERTS_PER_GROUP + e], v)
        cand.append(v)

    def argbest(vals):
        bv = vals[0]
        bi = jnp.zeros_like(sel)
        for e in range(1, len(vals)):
            better = vals[e] > bv
            bv = jnp.where(better, vals[e], bv)
            bi = jnp.where(better, e, bi)
        return bv, bi
    w0, i0 = argbest(cand)
    w1, i1 = argbest([jnp.where(i0 == e, -1.0, cand[e]) for e in range(EXPERTS_PER_GROUP)])
    tot = w0 + w1
    base = sel * EXPERTS_PER_GROUP
    return base + i0, base + i1, w0 / tot, w1 / tot


def _row_to_vmem(hbm, row, buf, r, sem):
    d = buf.shape[1]
    return pltpu.make_async_copy(hbm.at[pl.ds(pl.multiple_of(row * d, d), d)], buf.at[r], sem)


def _row_to_hbm(buf, r, hbm, row, sem):
    d = buf.shape[1]
    return pltpu.make_async_copy(buf.at[r], hbm.at[pl.ds(pl.multiple_of(row * d, d), d)], sem)


def _ln_kernel(*refs, alpha, pair_rows, emit_h, h_dtype, with_router):
    it = iter(refs)
    if pair_rows:
        slot_ref = next(it)
    x_ref, y_ref = next(it), next(it)
    if pair_rows:
        w_ref = next(it)
    gate_ref, g_ref, b_ref = next(it), next(it), next(it)
    if emit_h:
        sc_ref, sh_ref = next(it), next(it)
    if with_router:
        rwh_ref, rwl_ref, rb_ref = next(it), next(it), next(it)
    xo_ref = next(it)
    if emit_h:
        h_ref = next(it)
    if with_router:
        idx_ref, wt_ref, cnt_ref = next(it), next(it), next(it)
        hbufs, hsem = (next(it), next(it)), next(it)
    if pair_rows:
        ybufs, ysem = ((next(it), next(it)), (next(it), next(it))), next(it)
    tm = x_ref.shape[0]
    i = pl.program_id(0)
    n = pl.num_programs(0)

    def finish(y):
        xn = _layer_norm_rows(alpha * x_ref[...] + (1.0 + gate_ref[...]) * y, g_ref[...], b_ref[...])
        xo_ref[...] = xn
        if emit_h:
            h = xn * (1.0 + sc_ref[...]) + sh_ref[...]
            if not with_router:
                h_ref[...] = h.astype(h_dtype)
            return h

    if pair_rows:
        t_all = n * tm

        def row_copy(slot_row, s, k, r):
            return pltpu.make_async_copy(y_ref.at[pl.ds(slot_row, 1), :], ybufs[s][k].at[pl.ds(r, 1), :],
                                         ysem.at[s])

        def fetch(tile, s, r):
            row_copy(slot_ref[tile * tm + r], s, 0, r).start()
            row_copy(slot_ref[t_all + tile * tm + r], s, 1, r).start()

        def wait_tile(s):
            for r in range(tm):
                row_copy(0, s, 0, r).wait()
                row_copy(0, s, 1, r).wait()

        @pl.when(i == 0)
        def _():
            def first(r, c):
                fetch(0, 0, r)
                return c
            lax.fori_loop(0, tm, first, 0, unroll=8)

        def step(s):
            wait_tile(s)
            nxt = jnp.minimum(i + 1, n - 1)

            def issue(r, c):
                fetch(nxt, 1 - s, r)
                return c
            lax.fori_loop(0, tm, issue, 0, unroll=8)
            finish(ybufs[s][0][...] * w_ref[:, 0:1] + ybufs[s][1][...] * w_ref[:, 1:2])

            @pl.when(i == n - 1)
            def _():
                wait_tile(1 - s)
        for parity in range(2):
            pl.when(i % 2 == parity)(functools.partial(step, parity))
        return

    h = finish(y_ref[...])
    if with_router:
        h_hi = h.astype(BF16)
        h_lo = (h - h_hi.astype(F32)).astype(BF16)
        rwh = rwh_ref[...]
        logits = (_dot_nt(rwh, h_hi) + _dot_nt(rwh, h_lo) + _dot_nt(rwl_ref[...], h_hi)
                  + rb_ref[...])
        e = jnp.exp(logits - jnp.max(logits, axis=0, keepdims=True))
        p = e / jnp.sum(e, axis=0, keepdims=True)
        i0, i1, w0, w1 = _route_rows(p)

        @pl.when(i == 0)
        def _():
            cnt_ref[...] = jnp.zeros_like(cnt_ref)
        erow = lax.broadcasted_iota(I32, logits.shape, 0)
        oh0 = (erow == i0).astype(F32)
        oh1 = (erow == i1).astype(F32)
        both = oh0 + oh1
        tr = lax.broadcasted_iota(I32, (tm, tm), 0)
        tc = lax.broadcasted_iota(I32, (tm, tm), 1)
        before = cnt_ref[:, 0:1] + _dot(both.astype(BF16), (tr < tc).astype(BF16))
        r0 = jnp.sum(oh0 * before, axis=0, keepdims=True).astype(I32)
        r1 = jnp.sum(oh1 * before, axis=0, keepdims=True).astype(I32)
        cnt_ref[...] = cnt_ref[...] + jnp.sum(both, axis=1, keepdims=True)
        zi = jnp.zeros((4, tm), I32)
        idx_ref[...] = jnp.concatenate([i0, i1, r0, r1, zi], axis=0)
        wt_ref[...] = jnp.concatenate([w0, w1, jnp.zeros((6, tm), F32)], axis=0)

        def wait_sent(s):
            for r in range(tm):
                _row_to_hbm(hbufs[s], r, h_ref, 0, hsem.at[s]).wait()

        def send(s):
            @pl.when(i >= 2)
            def _():
                wait_sent(s)
            hbufs[s][...] = h
            for r in range(tm):
                _row_to_hbm(hbufs[s], r, h_ref, i * tm + r, hsem.at[s]).start()

            @pl.when(i == n - 1)
            def _():
                wait_sent(s)
                wait_sent(1 - s)
        for parity in range(2):
            pl.when(i % 2 == parity)(functools.partial(send, parity))


def _post_norm(x2, y, gate, ln_g, ln_b, alpha, seq, *, pair_slots=None, pair_w=None, scale=None,
               shift=None, h_dtype=BF16, router=None):
    t, d = x2.shape
    tm = _pick(seq, (128, 64))
    emit_h = scale is not None
    with_router = router is not None
    pair_rows = pair_slots is not None
    assert t // tm >= 2
    vec = pl.BlockSpec((None, 1, d), lambda i, *_: (i * tm // seq, 0, 0))
    par = pl.BlockSpec((1, d), lambda i, *_: (0, 0))
    rows = pl.BlockSpec((tm, d), lambda i, *_: (i, 0))
    hbm = pl.BlockSpec(memory_space=pl.ANY)
    in_specs = [rows, hbm if pair_rows else rows]
    args = [x2, y]
    if pair_rows:
        in_specs.append(pl.BlockSpec((tm, TOP_K), lambda i, *_: (i, 0)))
        args.append(pair_w)
    in_specs += [vec, par, par]
    args += [gate, ln_g.reshape(1, d), ln_b.reshape(1, d)]
    out_specs = [rows]
    out_shape = [jax.ShapeDtypeStruct((t, d), F32)]
    scratch = []
    if emit_h:
        in_specs += [vec, vec]
        args += [scale, shift]
        if with_router:
            out_specs.append(hbm)
            out_shape.append(jax.ShapeDtypeStruct((t * d,), h_dtype))
        else:
            out_specs.append(rows)
            out_shape.append(jax.ShapeDtypeStruct((t, d), h_dtype))
    if with_router:
        rwh, rwl, rb = router
        e = rwh.shape[0]
        in_specs += [pl.BlockSpec((e, d), lambda i, *_: (0, 0)), pl.BlockSpec((e, d), lambda i, *_: (0, 0)),
                     pl.BlockSpec((e, 1), lambda i, *_: (0, 0))]
        args += [rwh, rwl, rb]
        out_specs += [pl.BlockSpec((8, tm), lambda i, *_: (0, i))] * 2
        out_shape += [jax.ShapeDtypeStruct((8, t), I32), jax.ShapeDtypeStruct((8, t), F32)]
        scratch += [pltpu.VMEM((N_EXPERTS, LANES), F32), pltpu.VMEM((tm, d), h_dtype),
                    pltpu.VMEM((tm, d), h_dtype), pltpu.SemaphoreType.DMA((2,))]
    if pair_rows:
        scratch += [pltpu.VMEM((tm, d), F32)] * 4 + [pltpu.SemaphoreType.DMA((2,))]
        args = [pair_slots] + args
    grid_spec = pltpu.PrefetchScalarGridSpec(
        num_scalar_prefetch=1 if pair_rows else 0, grid=(t // tm,),
        in_specs=in_specs, out_specs=out_specs, scratch_shapes=scratch)
    return pl.pallas_call(
        functools.partial(_ln_kernel, alpha=alpha, pair_rows=pair_rows, emit_h=emit_h,
                          h_dtype=h_dtype, with_router=with_router),
        grid_spec=grid_spec, out_shape=out_shape,
        compiler_params=_cp(("arbitrary",)),
    )(*args)


def _moe_kernel(src_ref, exp_ref, h_hbm, wg_ref, wu_ref, wd_ref, y_ref, xbuf0, xbuf1, gsem):
    i = pl.program_id(0)
    nb = pl.num_programs(0)
    R = MOE_BLOCK
    xbufs = (xbuf0, xbuf1)

    def gather_row(block, s, r):
        _row_to_vmem(h_hbm, src_ref[block * R + r], xbufs[s], r, gsem.at[s]).start()

    def wait_gather(s):
        for r in range(R):
            _row_to_vmem(h_hbm, 0, xbufs[s], r, gsem.at[s]).wait()

    @pl.when(i == 0)
    def _():
        def first(r, c):
            gather_row(0, 0, r)
            return c
        lax.fori_loop(0, R, first, 0, unroll=8)

    def step(slot):
        other = 1 - slot
        wait_gather(slot)
        nxt = jnp.minimum(i + 1, nb - 1)

        def issue(r, c):
            gather_row(nxt, other, r)
            return c
        lax.fori_loop(0, R, issue, 0, unroll=8)
        x = xbufs[slot][...].astype(BF16)
        hid = _silu(_dot(x, wg_ref[...])) * _dot(x, wu_ref[...])
        y_ref[...] = _dot(hid.astype(BF16), wd_ref[...])

        @pl.when(i == nb - 1)
        def _():
            wait_gather(other)

    for parity in range(2):
        pl.when(i % 2 == parity)(functools.partial(step, parity))


def _moe_blocks(h_flat, slot_src, block_exp, wg, wu, wd):
    d = wg.shape[1]
    f = wg.shape[2]
    n_blocks = block_exp.shape[0]
    R = MOE_BLOCK
    grid_spec = pltpu.PrefetchScalarGridSpec(
        num_scalar_prefetch=2,
        grid=(n_blocks,),
        in_specs=[pl.BlockSpec(memory_space=pl.ANY),
                  pl.BlockSpec((None, d, f), lambda i, s, e: (e[i], 0, 0)),
                  pl.BlockSpec((None, d, f), lambda i, s, e: (e[i], 0, 0)),
                  pl.BlockSpec((None, f, d), lambda i, s, e: (e[i], 0, 0))],
        out_specs=pl.BlockSpec((R, d), lambda i, s, e: (i, 0)),
        scratch_shapes=[pltpu.VMEM((R, d), F32), pltpu.VMEM((R, d), F32),
                        pltpu.SemaphoreType.DMA((2,))],
    )
    return pl.pallas_call(
        _moe_kernel,
        grid_spec=grid_spec,
        out_shape=jax.ShapeDtypeStruct((n_blocks * R, d), F32),
        compiler_params=_cp(("arbitrary",)),
    )(slot_src, block_exp, h_flat, wg, wu, wd)


def _dispatch_plan(idx, t):
    n_assign = t * TOP_K
    e01 = idx[:TOP_K]
    rank01 = idx[TOP_K:2 * TOP_K]
    experts = jnp.arange(N_EXPERTS, dtype=I32)
    onehot = (e01[:, :, None] == experts).astype(I32)
    counts = jnp.sum(onehot, axis=(0, 1))
    padded = (counts + MOE_BLOCK - 1) // MOE_BLOCK * MOE_BLOCK
    pad_end = jnp.cumsum(padded)
    pad_start = pad_end - padded
    dest = jnp.sum(onehot * pad_start, axis=-1) + rank01
    n_blocks = -(-n_assign // MOE_BLOCK) + N_EXPERTS
    n_slots = n_blocks * MOE_BLOCK
    block_start = jnp.arange(n_blocks, dtype=I32) * MOE_BLOCK
    block_exp = jnp.minimum(jnp.sum(block_start[:, None] >= pad_end[None, :], -1), N_EXPERTS - 1).astype(I32)
    tok = jnp.arange(t, dtype=I32)
    slot_src = jnp.zeros((n_slots,), I32).at[dest.reshape(-1)].set(jnp.concatenate([tok, tok]))
    return slot_src, block_exp, dest.reshape(-1)


def kernel(x, c, positions, w_in, conv_w, conv_b, dt_bias, a_log, d_skip, ssd_norm_w, hgrn_gamma,
           hgrn_norm_w, ret_norm_w, ret_norm_b, w_out, ada_down, ada_up, ada_b, ln_g, ln_b,
           router_w, router_b, w_gate, w_up, w_down):
    bsz, seq, d = x.shape
    depth = w_in.shape[0]
    t = bsz * seq
    ssd_w = d // 2
    ssd_heads = ssd_w // SSD_HEAD_DIM
    conv_dim = ssd_w + 2 * SSD_GROUPS * SSD_STATE
    hg_w = d // 4
    ret_w = d // 4
    ret_kw = (ret_w // HEAD_V) * RET_KDIM
    alpha = (2.0 * depth) ** 0.25
    off_dt = ssd_w + conv_dim
    off_hq = off_dt
    off_rq = off_hq + 4 * hg_w

    p = jax.nn.softmax(hgrn_gamma.astype(F32), axis=0)
    lower_bounds = jnp.cumsum(p, axis=0) - p[0]
    mod = _modulation(c, ada_down, ada_up, ada_b)
    cos, sin = _rope_tables(positions)

    rwt = router_w.T
    rw_hi = rwt.astype(BF16)
    rw_lo = (rwt - rw_hi.astype(F32)).astype(BF16)
    rb = router_b.reshape(N_EXPERTS, 1).astype(F32)

    x2 = x.reshape(t, d)
    h = _modulate(x2, mod[0, 1], mod[0, 0], seq)
    for l in range(depth):
        shift1, scale1, gate1, shift2, scale2, gate2 = [mod[l, i] for i in range(N_MOD)]
        w_main = jnp.concatenate([w_in[l][:, :off_dt], w_in[l][:, off_dt + ssd_heads:]], axis=1).astype(BF16)
        w_dt = jnp.pad(w_in[l][:, off_dt:off_dt + ssd_heads], ((0, 0), (0, LANES - ssd_heads))).astype(BF16)
        proj = _matmul(h, w_main, F32)
        dt_raw = _matmul(h, w_dt, F32)
        y_ssd = _ssd(proj, dt_raw, conv_w[l], conv_b[l], dt_bias[l], a_log[l], d_skip[l], ssd_norm_w[l],
                     bsz, seq, ssd_w)
        y_hg = _hgrn(proj, lower_bounds[l], hgrn_norm_w[l], bsz, seq, hg_w, off_hq)
        y_ret = _retention(proj, cos, sin, ret_norm_w[l], ret_norm_b[l], bsz, seq, ret_w, off_rq)
        mixed = _out_proj(y_ssd, y_hg, y_ret, w_out[l].astype(BF16))
        x2, h2, idx, wts = _post_norm(x2, mixed, gate1, ln_g[l, 0], ln_b[l, 0], alpha, seq,
                                      scale=scale2, shift=shift2, h_dtype=F32, router=(rw_hi, rw_lo, rb))
        slot_src, block_exp, pair_slots = _dispatch_plan(idx, t)
        y2 = _moe_blocks(h2, slot_src, block_exp,
                         w_gate[l].astype(BF16), w_up[l].astype(BF16), w_down[l].astype(BF16))
        pair_w = wts[:TOP_K].T
        if l + 1 < depth:
            x2, h = _post_norm(x2, y2, gate2, ln_g[l, 1], ln_b[l, 1], alpha, seq, pair_slots=pair_slots,
                               pair_w=pair_w, scale=mod[l + 1, 1], shift=mod[l + 1, 0])
        else:
            (x2,) = _post_norm(x2, y2, gate2, ln_g[l, 1], ln_b[l, 1], alpha, seq, pair_slots=pair_slots,
                               pair_w=pair_w)
    return x2.reshape(bsz, seq, d)
```

```python
import functools
import math

import jax
import jax.numpy as jnp
import numpy as np
from jax import lax
from jax.experimental import pallas as pl
from jax.experimental.pallas import tpu as pltpu

F32 = jnp.float32
BF16 = jnp.bfloat16
I32 = jnp.int32

CHUNK = 64
TINY = 1e-30
NORM_EPS = 1e-5
ROPE_BASE = 10000.0
LANES = 128
SSD_HEAD_DIM = 64
SSD_GROUPS = 4
SSD_STATE = 128
SSD_CONV = 4
HEAD_V = 128
RET_KDIM = 64
N_EXPERTS = 16
N_EXPERT_GROUPS = 4
EXPERTS_PER_GROUP = N_EXPERTS // N_EXPERT_GROUPS
TOP_K = 2
MOE_BLOCK = 256
N_MOD = 6
VMEM_LIMIT = 56 * 1024 * 1024

SSD_L = 128
HG_L = 64
HG_SUB = 16
HG_SAFE_DECAY = 60.0
RET_L = 2 * CHUNK


def _cp(sem):
    return pltpu.CompilerParams(dimension_semantics=sem, vmem_limit_bytes=VMEM_LIMIT)


def _pick(n, cands):
    for c in cands:
        if n % c == 0:
            return c
    return n


def _silu(x):
    return x * jax.nn.sigmoid(x)


def _split3(v):
    hi = v.astype(BF16)
    r1 = v - hi.astype(F32)
    mid = r1.astype(BF16)
    lo = (r1 - mid.astype(F32)).astype(BF16)
    return hi, mid, lo


def _dot(a, b):
    return jnp.dot(a, b, preferred_element_type=F32)


def _dot_nt(a, b):
    return lax.dot_general(a, b, (((1,), (1,)), ((), ())), preferred_element_type=F32)


def _sel_right(v, m01):
    hi, mid, lo = _split3(v)
    return _dot(hi, m01) + _dot(mid, m01) + _dot(lo, m01)


def _sel_left(m01, v):
    hi, mid, lo = _split3(v)
    return _dot(m01, hi) + _dot(m01, mid) + _dot(m01, lo)


def _tri(n):
    r = lax.broadcasted_iota(I32, (n, n), 0)
    c = lax.broadcasted_iota(I32, (n, n), 1)
    return r >= c


def _mod_kernel(c_ref, down_ref, up_ref, b_ref, o_ref):
    c = c_ref[...]
    t = jnp.dot(_silu(c), down_ref[...], preferred_element_type=F32,
                precision=lax.Precision.HIGHEST)
    o_ref[...] = jnp.dot(t, up_ref[...], preferred_element_type=F32,
                         precision=lax.Precision.HIGHEST) + b_ref[...]


def _modulation(c, ada_down, ada_up, ada_b):
    depth, d, rank = ada_down.shape
    bsz = c.shape[0]
    rows = 8
    cp = jnp.zeros((rows, d), F32).at[:bsz].set(c)
    out = pl.pallas_call(
        _mod_kernel,
        grid=(depth, N_MOD),
        in_specs=[pl.BlockSpec((rows, d), lambda l, j: (0, 0)),
                  pl.BlockSpec((None, d, rank), lambda l, j: (l, 0, 0)),
                  pl.BlockSpec((None, rank, d), lambda l, j: (l, 0, j)),
                  pl.BlockSpec((None, 1, d), lambda l, j: (l, 0, j))],
        out_specs=pl.BlockSpec((None, None, rows, d), lambda l, j: (l, j, 0, 0)),
        out_shape=jax.ShapeDtypeStruct((depth, N_MOD, rows, d), F32),
        compiler_params=_cp(("arbitrary", "arbitrary")),
    )(cp, ada_down, ada_up, ada_b.reshape(depth, 1, N_MOD * d))
    return out[:, :, :bsz, None, :]


def _modulate_kernel(x_ref, sc_ref, sh_ref, h_ref):
    h_ref[...] = (x_ref[...] * (1.0 + sc_ref[...]) + sh_ref[...]).astype(h_ref.dtype)


def _modulate(x2, scale, shift, seq):
    t, d = x2.shape
    tm = _pick(seq, (512, 256, 128, 64))
    vec = pl.BlockSpec((None, 1, d), lambda i: (i * tm // seq, 0, 0))
    return pl.pallas_call(
        _modulate_kernel,
        grid=(t // tm,),
        in_specs=[pl.BlockSpec((tm, d), lambda i: (i, 0)), vec, vec],
        out_specs=pl.BlockSpec((tm, d), lambda i: (i, 0)),
        out_shape=jax.ShapeDtypeStruct((t, d), BF16),
        compiler_params=_cp(("arbitrary",)),
    )(x2, scale, shift)


def _mm_kernel(a_ref, b_ref, o_ref):
    o_ref[...] = _dot(a_ref[...], b_ref[...]).astype(o_ref.dtype)


def _matmul(a, b, out_dtype):
    m, k = a.shape
    n = b.shape[1]
    tm = _pick(m, (1024, 512, 256, 128))
    tn = _pick(n, (1024, 768, 512, 384, 256, 128))
    return pl.pallas_call(
        _mm_kernel,
        grid=(m // tm, n // tn),
        in_specs=[pl.BlockSpec((tm, k), lambda i, j: (i, 0)),
                  pl.BlockSpec((k, tn), lambda i, j: (0, j))],
        out_specs=pl.BlockSpec((tm, tn), lambda i, j: (i, j)),
        out_shape=jax.ShapeDtypeStruct((m, n), out_dtype),
        compiler_params=_cp(("arbitrary", "arbitrary")),
    )(a, b)


def _mm3_kernel(a1_ref, a2_ref, a3_ref, b1_ref, b2_ref, b3_ref, o_ref):
    o_ref[...] = (_dot(a1_ref[...], b1_ref[...]) + _dot(a2_ref[...], b2_ref[...])
                  + _dot(a3_ref[...], b3_ref[...]))


def _out_proj(y_ssd, y_hg, y_ret, w_out):
    m = y_ssd.shape[0]
    d = w_out.shape[1]
    k1, k2, k3 = y_ssd.shape[1], y_hg.shape[1], y_ret.shape[1]
    tm = _pick(m, (1024, 512, 256, 128))
    tn = _pick(d, (1024, 512, 256, 128))
    return pl.pallas_call(
        _mm3_kernel,
        grid=(m // tm, d // tn),
        in_specs=[pl.BlockSpec((tm, k1), lambda i, j: (i, 0)),
                  pl.BlockSpec((tm, k2), lambda i, j: (i, 0)),
                  pl.BlockSpec((tm, k3), lambda i, j: (i, 0)),
                  pl.BlockSpec((k1, tn), lambda i, j: (0, j)),
                  pl.BlockSpec((k2, tn), lambda i, j: (k1 // k2, j)),
                  pl.BlockSpec((k3, tn), lambda i, j: ((k1 + k2) // k3, j))],
        out_specs=pl.BlockSpec((tm, tn), lambda i, j: (i, j)),
        out_shape=jax.ShapeDtypeStruct((m, d), F32),
        compiler_params=_cp(("arbitrary", "arbitrary")),
    )(y_ssd, y_hg, y_ret, w_out, w_out, w_out)


def _ssd_kernel(z_ref, xs_ref, b_ref, c_ref, dt_ref,
                wx_ref, wb_ref, wc_ref, bx_ref, bb_ref, bc_ref,
                dtb_ref, alog_ref, dskip_ref, nw_ref, e_ref, sel_ref,
                o_ref, st_ref, px_ref, pb_ref, pc_ref, *, heads_per_group):
    L = SSD_L
    n = SSD_STATE
    groups = sel_ref.shape[0]
    gw = o_ref.shape[1] // groups
    ci = pl.program_id(1)

    @pl.when(ci == 0)
    def _():
        st_ref[...] = jnp.zeros_like(st_ref)
        px_ref[0:8, :] = jnp.zeros((8, px_ref.shape[1]), F32)
        pb_ref[0:8, :] = jnp.zeros((8, pb_ref.shape[1]), F32)
        pc_ref[0:8, :] = jnp.zeros((8, pc_ref.shape[1]), F32)

    def conv_silu(raw_ref, pad_ref, w_ref, bias_ref):
        pad_ref[8:8 + L, :] = raw_ref[...]
        acc = bias_ref[...]
        for k in range(SSD_CONV):
            acc = acc + w_ref[k:k + 1, :] * pad_ref[pl.ds(8 - (SSD_CONV - 1) + k, L), :]
        pad_ref[0:8, :] = pad_ref[L:L + 8, :]
        return _silu(acc)

    xs_all = conv_silu(xs_ref, px_ref, wx_ref, bx_ref)
    bm_all = conv_silu(b_ref, pb_ref, wb_ref, bb_ref)
    cm_all = conv_silu(c_ref, pc_ref, wc_ref, bc_ref)

    dt = jax.nn.softplus(dt_ref[...] + dtb_ref[...])
    dta = dt * (-jnp.exp(alog_ref[...]))
    tri = _tri(L)
    cum = _sel_left(tri.astype(BF16), dta)
    e01 = e_ref[...]
    dt_full_all = _sel_right(dt, e01)
    cum_full_all = _sel_right(cum, e01)
    lane = lax.broadcasted_iota(I32, (L, LANES), 1)
    lo_half = lane < SSD_HEAD_DIM

    for g in range(groups):
        sl = slice(g * gw, (g + 1) * gw)
        xs = xs_all[:, sl]
        bm = bm_all[:, g * n:(g + 1) * n]
        cm = cm_all[:, g * n:(g + 1) * n]
        cum_g = _sel_right(cum, sel_ref[g])
        cum_gt = cum_g.T
        cum_full = cum_full_all[:, sl]
        last = cum_full[L - 1:L, :]
        to_end = jnp.exp(last - cum_full)
        from_start = jnp.exp(cum_full)
        xdt = xs * dt_full_all[:, sl]

        bm16 = bm.astype(BF16)
        cm16 = cm.astype(BF16)
        cb = _dot_nt(cm16, bm16)
        ys = []
        for j in range(heads_per_group // 2):
            atts = []
            for i in (2 * j, 2 * j + 1):
                dec = jnp.where(tri, jnp.exp(jnp.minimum(cum_g[:, i:i + 1] - cum_gt[i:i + 1, :], 0.0)), 0.0)
                atts.append((cb * dec).astype(BF16))
            xp = xdt[:, j * LANES:(j + 1) * LANES]
            stack = jnp.concatenate([jnp.where(lo_half, xp, 0.0), jnp.where(lo_half, 0.0, xp)], axis=0)
            ys.append(_dot(jnp.concatenate(atts, axis=1), stack.astype(BF16)))
        y = jnp.concatenate(ys, axis=1) if len(ys) > 1 else ys[0]

        st = st_ref[g]
        y = y + _dot(cm16, st.astype(BF16)) * from_start
        st_ref[g] = st * jnp.exp(last) + _dot(bm.T.astype(BF16), (xdt * to_end).astype(BF16))

        y = y + xs * dskip_ref[:, sl]
        y = y * _silu(z_ref[:, sl])
        y = y * lax.rsqrt(jnp.mean(jnp.square(y), axis=-1, keepdims=True) + NORM_EPS) * nw_ref[:, sl]
        o_ref[:, sl] = y.astype(o_ref.dtype)


def _ssd(proj, dt_raw, conv_w, conv_b, dt_bias, a_log, d_skip, norm_w, bsz, seq, ssd_w):
    t = proj.shape[0]
    L = SSD_L
    nc = seq // L
    g = SSD_GROUPS
    gw = ssd_w // g
    n = SSD_STATE
    gn = g * n
    heads = ssd_w // SSD_HEAD_DIM
    r = heads // g
    assert r % 2 == 0 and gw % LANES == 0 and seq % L == 0 and heads <= LANES and ssd_w % gn == 0
    off_b = 2 * ssd_w
    off_c = off_b + gn
    hpad = LANES - heads
    dtb = jnp.pad(dt_bias, (0, hpad)).reshape(1, LANES)
    alog = jnp.pad(a_log, (0, hpad)).reshape(1, LANES)
    dskip = jnp.repeat(d_skip, SSD_HEAD_DIM).reshape(1, ssd_w)
    hid = jnp.arange(LANES)
    e01 = (hid[:, None] == (jnp.arange(ssd_w) // SSD_HEAD_DIM)[None, :]).astype(BF16)
    sel = (hid[None, :, None] == (jnp.arange(g)[:, None, None] * r + hid[None, None, :])
           ) & (hid[None, None, :] < r)
    sel = sel.astype(BF16)
    row = lambda b, c: b * nc + c
    cw = conv_w
    cb2 = conv_b.reshape(1, -1)
    fixed = lambda shape, col: pl.BlockSpec(shape, lambda b, c: (0, col))
    in_specs = [
        pl.BlockSpec((L, ssd_w), lambda b, c: (row(b, c), 0)),
        pl.BlockSpec((L, ssd_w), lambda b, c: (row(b, c), 1)),
        pl.BlockSpec((L, gn), lambda b, c: (row(b, c), off_b // gn)),
        pl.BlockSpec((L, gn), lambda b, c: (row(b, c), off_c // gn)),
        pl.BlockSpec((L, LANES), lambda b, c: (row(b, c), 0)),
        fixed((SSD_CONV, ssd_w), 0),
        fixed((SSD_CONV, gn), ssd_w // gn),
        fixed((SSD_CONV, gn), ssd_w // gn + 1),
        fixed((1, ssd_w), 0),
        fixed((1, gn), ssd_w // gn),
        fixed((1, gn), ssd_w // gn + 1),
        fixed((1, LANES), 0),
        fixed((1, LANES), 0),
        fixed((1, ssd_w), 0),
        fixed((1, ssd_w), 0),
        fixed((LANES, ssd_w), 0),
        pl.BlockSpec((g, LANES, LANES), lambda b, c: (0, 0, 0)),
    ]
    return pl.pallas_call(
        functools.partial(_ssd_kernel, heads_per_group=r),
        grid=(bsz, nc),
        in_specs=in_specs,
        out_specs=pl.BlockSpec((L, ssd_w), lambda b, c: (row(b, c), 0)),
        out_shape=jax.ShapeDtypeStruct((t, ssd_w), BF16),
        scratch_shapes=[pltpu.VMEM((g, n, gw), F32),
                        pltpu.VMEM((L + 8, ssd_w), F32),
                        pltpu.VMEM((L + 8, gn), F32),
                        pltpu.VMEM((L + 8, gn), F32)],
        compiler_params=_cp(("arbitrary", "arbitrary")),
    )(proj, proj, proj, proj, dt_raw, cw, cw, cw, cb2, cb2, cb2,
      dtb, alog, dskip, norm_w.reshape(1, ssd_w), e01, sel)


def _hgrn_kernel(q_ref, f_ref, v_ref, g_ref, lb_ref, nw_ref, o_ref, st_ref):
    L = HG_L
    SB = HG_SUB
    nsb = L // SB
    rows = q_ref.shape[0]
    ci = pl.program_id(2)

    @pl.when(ci == 0)
    def _():
        st_ref[...] = jnp.zeros_like(st_ref)

    lb = lb_ref[...]
    nw = nw_ref[...]
    tri = _tri(L).astype(BF16)
    ones = jnp.ones((LANES, LANES), BF16)
    zpad = jnp.zeros((LANES - L, LANES), F32)
    lane = lax.broadcasted_iota(I32, (SB, LANES), 1)
    rowi = lax.broadcasted_iota(I32, (SB, LANES), 0)

    def log_decay(fr):
        return jnp.log(jnp.maximum(lb + (1.0 - lb) * jax.nn.sigmoid(fr), TINY))

    sub_decay = jnp.sum((-log_decay(f_ref[...])).reshape(rows // SB, SB, LANES), axis=1)
    small_decay = jnp.max(sub_decay) < HG_SAFE_DECAY

    def scores_factored(q, key, bcum):
        att_rows = []
        for i in range(nsb):
            lo, hi = i * SB, (i + 1) * SB
            edge = bcum[lo - 1:lo] if i > 0 else jnp.zeros((1, LANES), F32)
            qs = (q[lo:hi] * jnp.exp(bcum[lo:hi] - edge)).astype(BF16)
            ks = key[:hi] * jnp.exp(edge - bcum[:hi])
            ks = jnp.concatenate([ks, jnp.zeros((LANES - hi, LANES), F32)], axis=0).astype(BF16)
            att_rows.append(jnp.where(rowi + lo >= lane, _dot_nt(qs, ks), 0.0))
        return jnp.concatenate(att_rows, axis=0)

    def scores_exact(q, key, bcum):
        pieces = []
        for i in range(nsb):
            qi = q[i * SB:(i + 1) * SB]
            bi = bcum[i * SB:(i + 1) * SB]
            for j in range(SB):
                r = i * SB + j
                pieces.append((qi * key[r:r + 1] * jnp.exp(jnp.minimum(bi - bcum[r:r + 1], 0.0))).astype(BF16))
        sums = _dot(jnp.concatenate(pieces, axis=0), ones)

        att_rows = []
        for i in range(nsb):
            diag = jnp.zeros((SB, LANES), F32)
            for j in range(SB):
                r = i * SB + j
                diag = jnp.where(lane == r, sums[r * SB:(r + 1) * SB], diag)
            att_i = jnp.where(rowi + i * SB >= lane, diag, 0.0)
            if i > 0:
                edge = bcum[i * SB - 1:i * SB]
                qs = (q[i * SB:(i + 1) * SB] * jnp.exp(bcum[i * SB:(i + 1) * SB] - edge)).astype(BF16)
                ks = jnp.concatenate([key * jnp.exp(jnp.minimum(edge - bcum, 0.0)), zpad], axis=0).astype(BF16)
                att_i = jnp.where(lane < i * SB, _dot_nt(qs, ks), att_i)
            att_rows.append(att_i)
        return jnp.concatenate(att_rows, axis=0)

    def chunk(c, scores):
        r0 = c * L
        q = q_ref[r0:r0 + L, :]
        fr = f_ref[r0:r0 + L, :]
        v = v_ref[r0:r0 + L, :]
        log_f = log_decay(fr)
        key = (1.0 - lb) * jax.nn.sigmoid(-fr)
        bcum = _sel_left(tri, log_f)
        last = bcum[L - 1:L, :]
        att = scores(q, key, bcum)

        vpad = jnp.concatenate([v, zpad], axis=0)
        st = st_ref[...]
        o = (_dot_nt((q * jnp.exp(bcum)).astype(BF16), st.astype(BF16))
             + _dot(att.astype(BF16), vpad.astype(BF16)))
        kk = jnp.concatenate([key * jnp.exp(last - bcum), zpad], axis=0)
        st_ref[...] = st * jnp.exp(last) + _dot(vpad.T.astype(BF16), kk.astype(BF16))

        o = o * lax.rsqrt(jnp.mean(jnp.square(o), axis=-1, keepdims=True) + NORM_EPS) * nw
        o_ref[r0:r0 + L, :] = (o * _silu(g_ref[r0:r0 + L, :])).astype(o_ref.dtype)

    @pl.when(small_decay)
    def _():
        for c in range(rows // L):
            chunk(c, scores_factored)

    @pl.when(jnp.logical_not(small_decay))
    def _():
        for c in range(rows // L):
            chunk(c, scores_exact)


def _hgrn(proj, lower_bound, norm_w, bsz, seq, hg_w, off_q):
    t = proj.shape[0]
    rows = _pick(seq, (8 * HG_L, 4 * HG_L, 2 * HG_L, HG_L))
    nc = seq // rows
    h = hg_w // HEAD_V
    cq = off_q // LANES
    row = lambda b, hi, c: b * nc + c
    blk = lambda k: pl.BlockSpec((rows, LANES), lambda b, hi, c: (row(b, hi, c), cq + k * h + hi))
    vec = pl.BlockSpec((1, LANES), lambda b, hi, c: (0, hi))
    return pl.pallas_call(
        _hgrn_kernel,
        grid=(bsz, h, nc),
        in_specs=[blk(0), blk(1), blk(2), blk(3), vec, vec],
        out_specs=pl.BlockSpec((rows, LANES), lambda b, hi, c: (row(b, hi, c), hi)),
        out_shape=jax.ShapeDtypeStruct((t, hg_w), BF16),
        scratch_shapes=[pltpu.VMEM((HEAD_V, LANES), F32)],
        compiler_params=_cp(("arbitrary", "arbitrary", "arbitrary")),
    )(proj, proj, proj, proj, lower_bound.reshape(1, hg_w), norm_w.reshape(1, hg_w))


def _rope_kernel(pos_ref, freq_ref, cos_ref, sin_ref):
    ang = pos_ref[...].astype(F32) * freq_ref[...]
    lane = lax.broadcasted_iota(I32, ang.shape, 1)
    first = (lane % RET_KDIM) < (RET_KDIM // 2)
    cos_ref[...] = jnp.cos(ang)
    s = jnp.sin(ang)
    sin_ref[...] = jnp.where(first, -s, s)


def _rope_tables(positions):
    t = positions.size
    half = RET_KDIM // 2
    freq = ROPE_BASE ** (-jnp.arange(half, dtype=F32) / half)
    freq = jnp.tile(freq, LANES // half).reshape(1, LANES)
    tm = _pick(t, (512, 256, 128))
    return pl.pallas_call(
        _rope_kernel,
        grid=(t // tm,),
        in_specs=[pl.BlockSpec((tm, 1), lambda i: (i, 0)),
                  pl.BlockSpec((1, LANES), lambda i: (0, 0))],
        out_specs=[pl.BlockSpec((tm, LANES), lambda i: (i, 0))] * 2,
        out_shape=[jax.ShapeDtypeStruct((t, LANES), F32)] * 2,
        compiler_params=_cp(("arbitrary",)),
    )(positions.reshape(t, 1), freq)


def _ret_kernel(q_ref, k_ref, v_ref, g_ref, cos_ref, sin_ref, dm_ref, te_ref, fs_ref, cd_ref,
                nw_ref, nb_ref, o_ref, st_ref):
    L = RET_L
    n_pairs = te_ref.shape[0]
    ci = pl.program_id(2)

    @pl.when(ci == 0)
    def _():
        st_ref[...] = jnp.zeros_like(st_ref)

    cos = cos_ref[...]
    sin = sin_ref[...]
    lane = lax.broadcasted_iota(I32, (L, LANES), 1)
    rowi = lax.broadcasted_iota(I32, (L, LANES), 0)
    first = (lane % RET_KDIM) < (RET_KDIM // 2)

    def rope(x):
        sw = jnp.where(first, pltpu.roll(x, LANES - RET_KDIM // 2, 1), pltpu.roll(x, RET_KDIM // 2, 1))
        return x * cos + sw * sin

    for p in range(n_pairs):
        qr = rope(q_ref[:, p * LANES:(p + 1) * LANES])
        kr = rope(k_ref[:, p * LANES:(p + 1) * LANES]) * RET_KDIM ** -0.5
        kr16 = kr.astype(BF16)
        v16 = v_ref[:, 2 * p * HEAD_V:2 * (p + 1) * HEAD_V].astype(BF16)

        kt = (kr * te_ref[p]).T
        cd = cd_ref[p]

        def summary(frames):
            full = _dot(jnp.where(frames, kt, 0.0).astype(BF16), v16)
            return jnp.where(rowi < RET_KDIM, full[:, :HEAD_V], full[:, HEAD_V:])
        s0 = st_ref[p]
        s1 = s0 * cd + summary(lane < CHUNK)
        st_ref[p] = s1 * cd + summary(lane >= CHUNK)
        s0_16 = s0.astype(BF16)
        s1_16 = s1.astype(BF16)

        for hh in range(2):
            head = 2 * p + hh
            qh = jnp.where((lane // RET_KDIM) == hh, qr, 0.0).astype(BF16)
            scores = _dot_nt(qh, kr16) * dm_ref[head]
            o = _dot(scores.astype(BF16), v16[:, hh * HEAD_V:(hh + 1) * HEAD_V])
            inter = jnp.where(rowi < CHUNK, _dot(qh, s0_16), _dot(qh, s1_16))
            o = o + inter * fs_ref[head]
            mu = jnp.mean(o, axis=-1, keepdims=True)
            var = jnp.mean(jnp.square(o - mu), axis=-1, keepdims=True)
            sl = slice(head * HEAD_V, (head + 1) * HEAD_V)
            o = (o - mu) * lax.rsqrt(var + NORM_EPS) * nw_ref[:, sl] + nb_ref[:, sl]
            o_ref[:, sl] = (o * _silu(g_ref[:, sl])).astype(o_ref.dtype)


def _retention(proj, cos, sin, norm_w, norm_b, bsz, seq, ret_w, off_q):
    t = proj.shape[0]
    L = RET_L
    nc = seq // L
    h = ret_w // HEAD_V
    pairs = h // 2
    assert h % 2 == 0 and seq % L == 0
    kw = h * RET_KDIM
    off_k = off_q + kw
    off_v = off_k + kw
    off_g = off_v + ret_w
    assert off_v % (2 * HEAD_V) == 0 and off_g % (2 * HEAD_V) == 0
    log_gamma = jnp.log1p(-jnp.exp2(-5.0 - jnp.arange(h, dtype=F32)))
    pos = jnp.arange(CHUNK, dtype=F32)
    d_intra = jnp.exp(log_gamma[:, None, None] * jnp.abs(pos[:, None] - pos[None, :]))
    dm = jnp.zeros((h, L, L), F32)
    dm = dm.at[:, :CHUNK, :CHUNK].set(d_intra).at[:, CHUNK:, CHUNK:].set(d_intra)
    to_end = jnp.exp(log_gamma[:, None] * (CHUNK - 1 - pos))
    te = jnp.repeat(jnp.tile(to_end, (1, 2)).reshape(pairs, 2, L), RET_KDIM, axis=1)
    te = jnp.swapaxes(te, 1, 2)
    from_start = jnp.exp(log_gamma[:, None] * (pos + 1.0))
    fs = jnp.broadcast_to(jnp.tile(from_start, (1, 2))[:, :, None], (h, L, HEAD_V))
    cdec = jnp.exp(log_gamma * CHUNK)
    cd = jnp.broadcast_to(jnp.repeat(cdec.reshape(pairs, 2), RET_KDIM, axis=1)[:, :, None],
                          (pairs, 2 * RET_KDIM, HEAD_V))
    n_p = _pick(pairs, (4, 2, 1))
    wq = n_p * LANES
    wv = n_p * 2 * HEAD_V
    assert off_q % wq == 0 and off_k % wq == 0 and off_v % wv == 0 and off_g % wv == 0
    row = lambda b, p, c: b * nc + c
    tab = lambda n0, r: pl.BlockSpec((n0, r, LANES), lambda b, p, c: (p, 0, 0))
    return pl.pallas_call(
        _ret_kernel,
        grid=(bsz, pairs // n_p, nc),
        in_specs=[pl.BlockSpec((L, wq), lambda b, p, c: (row(b, p, c), off_q // wq + p)),
                  pl.BlockSpec((L, wq), lambda b, p, c: (row(b, p, c), off_k // wq + p)),
                  pl.BlockSpec((L, wv), lambda b, p, c: (row(b, p, c), off_v // wv + p)),
                  pl.BlockSpec((L, wv), lambda b, p, c: (row(b, p, c), off_g // wv + p)),
                  pl.BlockSpec((L, LANES), lambda b, p, c: (row(b, p, c), 0)),
                  pl.BlockSpec((L, LANES), lambda b, p, c: (row(b, p, c), 0)),
                  tab(2 * n_p, L),
                  tab(n_p, L),
                  tab(2 * n_p, L),
                  tab(n_p, 2 * RET_KDIM),
                  pl.BlockSpec((1, wv), lambda b, p, c: (0, p)),
                  pl.BlockSpec((1, wv), lambda b, p, c: (0, p))],
        out_specs=pl.BlockSpec((L, wv), lambda b, p, c: (row(b, p, c), p)),
        out_shape=jax.ShapeDtypeStruct((t, ret_w), BF16),
        scratch_shapes=[pltpu.VMEM((n_p, 2 * RET_KDIM, HEAD_V), F32)],
        compiler_params=_cp(("arbitrary", "arbitrary", "arbitrary")),
    )(proj, proj, proj, proj, cos, sin, dm, te, fs, cd,
      norm_w.reshape(1, ret_w), norm_b.reshape(1, ret_w))


def _layer_norm_rows(v, g, b):
    mu = jnp.mean(v, axis=-1, keepdims=True)
    var = jnp.mean(jnp.square(v - mu), axis=-1, keepdims=True)
    return (v - mu) * lax.rsqrt(var + NORM_EPS) * g + b


def _route_rows(p):
    rows = [p[e:e + 1, :] for e in range(N_EXPERTS)]
    scores = []
    for gi in range(N_EXPERT_GROUPS):
        a, b, c, d = rows[gi * EXPERTS_PER_GROUP:(gi + 1) * EXPERTS_PER_GROUP]
        hi1, lo1 = jnp.maximum(a, b), jnp.minimum(a, b)
        hi2, lo2 = jnp.maximum(c, d), jnp.minimum(c, d)
        scores.append(jnp.maximum(hi1, hi2) + jnp.maximum(jnp.minimum(hi1, hi2), jnp.maximum(lo1, lo2)))
    best = scores[0]
    sel = jnp.zeros_like(best, dtype=I32)
    for gi in range(1, N_EXPERT_GROUPS):
        better = scores[gi] > best
        best = jnp.where(better, scores[gi], best)
        sel = jnp.where(better, gi, sel)
    cand = []
    for e in range(EXPERTS_PER_GROUP):
        v = rows[e]
        for gi in range(1, N_EXPERT_GROUPS):
            v = jnp.where(sel == gi, rows[gi * EXPERTS_PER_GROUP + e], v)
        cand.append(v)

    def argbest(vals):
        bv = vals[0]
        bi = jnp.zeros_like(sel)
        for e in range(1, len(vals)):
            better = vals[e] > bv
            bv = jnp.where(better, vals[e], bv)
            bi = jnp.where(better, e, bi)
        return bv, bi
    w0, i0 = argbest(cand)
    w1, i1 = argbest([jnp.where(i0 == e, -1.0, cand[e]) for e in range(EXPERTS_PER_GROUP)])
    tot = w0 + w1
    base = sel * EXPERTS_PER_GROUP
    return base + i0, base + i1, w0 / tot, w1 / tot


def _row_to_vmem(hbm, row, buf, r, sem):
    return pltpu.make_async_copy(hbm.at[pl.ds(row, 1), :], buf.at[pl.ds(r, 1), :], sem)


def _ln_kernel(*refs, alpha, pair_rows, emit_h, h_dtype, with_router):
    it = iter(refs)
    if pair_rows:
        slot_ref = next(it)
    x_ref, y_ref = next(it), next(it)
    if pair_rows:
        w_ref = next(it)
    gate_ref, g_ref, b_ref = next(it), next(it), next(it)
    if emit_h:
        sc_ref, sh_ref = next(it), next(it)
    if with_router:
        rwh_ref, rwl_ref, rb_ref = next(it), next(it), next(it)
    xo_ref = next(it)
    if emit_h:
        h_ref = next(it)
    if with_router:
        idx_ref, wt_ref, cnt_ref = next(it), next(it), next(it)
    if pair_rows:
        ybufs, ysem = ((next(it), next(it)), (next(it), next(it))), next(it)
    tm = x_ref.shape[0]
    i = pl.program_id(0)
    n = pl.num_programs(0)

    def finish(y):
        xn = _layer_norm_rows(alpha * x_ref[...] + (1.0 + gate_ref[...]) * y, g_ref[...], b_ref[...])
        xo_ref[...] = xn
        if emit_h:
            h = xn * (1.0 + sc_ref[...]) + sh_ref[...]
            h_ref[...] = h.astype(h_dtype)
            return h

    if pair_rows:
        t_all = n * tm

        def row_copy(slot_row, s, k, r):
            return _row_to_vmem(y_ref, slot_row, ybufs[s][k], r, ysem.at[s])

        def fetch(tile, s, r):
            row_copy(slot_ref[tile * tm + r], s, 0, r).start()
            row_copy(slot_ref[t_all + tile * tm + r], s, 1, r).start()

        def wait_tile(s):
            for r in range(tm):
                row_copy(0, s, 0, r).wait()
                row_copy(0, s, 1, r).wait()

        @pl.when(i == 0)
        def _():
            def first(r, c):
                fetch(0, 0, r)
                return c
            lax.fori_loop(0, tm, first, 0, unroll=8)

        def step(s):
            wait_tile(s)
            nxt = jnp.minimum(i + 1, n - 1)

            def issue(r, c):
                fetch(nxt, 1 - s, r)
                return c
            lax.fori_loop(0, tm, issue, 0, unroll=8)
            finish(ybufs[s][0][...] * w_ref[:, 0:1] + ybufs[s][1][...] * w_ref[:, 1:2])

            @pl.when(i == n - 1)
            def _():
                wait_tile(1 - s)
        for parity in range(2):
            pl.when(i % 2 == parity)(functools.partial(step, parity))
        return

    h = finish(y_ref[...])
    if with_router:
        h_hi = h.astype(BF16)
        h_lo = (h - h_hi.astype(F32)).astype(BF16)
        rwh = rwh_ref[...]
        logits = (_dot_nt(rwh, h_hi) + _dot_nt(rwh, h_lo) + _dot_nt(rwl_ref[...], h_hi)
                  + rb_ref[...])
        e = jnp.exp(logits - jnp.max(logits, axis=0, keepdims=True))
        p = e / jnp.sum(e, axis=0, keepdims=True)
        i0, i1, w0, w1 = _route_rows(p)

        @pl.when(i == 0)
        def _():
            cnt_ref[...] = jnp.zeros_like(cnt_ref)
        erow = lax.broadcasted_iota(I32, logits.shape, 0)
        oh0 = (erow == i0).astype(F32)
        oh1 = (erow == i1).astype(F32)
        both = oh0 + oh1
        tr = lax.broadcasted_iota(I32, (tm, tm), 0)
        tc = lax.broadcasted_iota(I32, (tm, tm), 1)
        before = cnt_ref[:, 0:1] + _dot(both.astype(BF16), (tr < tc).astype(BF16))
        r0 = jnp.sum(oh0 * before, axis=0, keepdims=True).astype(I32)
        r1 = jnp.sum(oh1 * before, axis=0, keepdims=True).astype(I32)
        cnt_ref[...] = cnt_ref[...] + jnp.sum(both, axis=1, keepdims=True)
        zi = jnp.zeros((4, tm), I32)
        idx_ref[...] = jnp.concatenate([i0, i1, r0, r1, zi], axis=0)
        wt_ref[...] = jnp.concatenate([w0, w1, jnp.zeros((6, tm), F32)], axis=0)


def _post_norm(x2, y, gate, ln_g, ln_b, alpha, seq, *, pair_slots=None, pair_w=None, scale=None,
               shift=None, h_dtype=BF16, router=None):
    t, d = x2.shape
    tm = _pick(seq, (128, 64))
    emit_h = scale is not None
    with_router = router is not None
    pair_rows = pair_slots is not None
    assert t // tm >= 2
    vec = pl.BlockSpec((None, 1, d), lambda i, *_: (i * tm // seq, 0, 0))
    par = pl.BlockSpec((1, d), lambda i, *_: (0, 0))
    rows = pl.BlockSpec((tm, d), lambda i, *_: (i, 0))
    hbm = pl.BlockSpec(memory_space=pl.ANY)
    in_specs = [rows, hbm if pair_rows else rows]
    args = [x2, y]
    if pair_rows:
        in_specs.append(pl.BlockSpec((tm, TOP_K), lambda i, *_: (i, 0)))
        args.append(pair_w)
    in_specs += [vec, par, par]
    args += [gate, ln_g.reshape(1, d), ln_b.reshape(1, d)]
    out_specs = [rows]
    out_shape = [jax.ShapeDtypeStruct((t, d), F32)]
    scratch = []
    if emit_h:
        in_specs += [vec, vec]
        args += [scale, shift]
        out_specs.append(rows)
        out_shape.append(jax.ShapeDtypeStruct((t, d), h_dtype))
    if with_router:
        rwh, rwl, rb = router
        e = rwh.shape[0]
        in_specs += [pl.BlockSpec((e, d), lambda i, *_: (0, 0)), pl.BlockSpec((e, d), lambda i, *_: (0, 0)),
                     pl.BlockSpec((e, 1), lambda i, *_: (0, 0))]
        args += [rwh, rwl, rb]
        out_specs += [pl.BlockSpec((8, tm), lambda i, *_: (0, i))] * 2
        out_shape += [jax.ShapeDtypeStruct((8, t), I32), jax.ShapeDtypeStruct((8, t), F32)]
        scratch += [pltpu.VMEM((N_EXPERTS, LANES), F32)]
    if pair_rows:
        scratch += [pltpu.VMEM((tm, d), F32)] * 4 + [pltpu.SemaphoreType.DMA((2,))]
        args = [pair_slots] + args
    grid_spec = pltpu.PrefetchScalarGridSpec(
        num_scalar_prefetch=1 if pair_rows else 0, grid=(t // tm,),
        in_specs=in_specs, out_specs=out_specs, scratch_shapes=scratch)
    return pl.pallas_call(
        functools.partial(_ln_kernel, alpha=alpha, pair_rows=pair_rows, emit_h=emit_h,
                          h_dtype=h_dtype, with_router=with_router),
        grid_spec=grid_spec, out_shape=out_shape,
        compiler_params=_cp(("arbitrary",)),
    )(*args)


def _moe_kernel(src_ref, exp_ref, h_hbm, wg_ref, wu_ref, wd_ref, y_ref, xbuf0, xbuf1, gsem):
    i = pl.program_id(0)
    nb = pl.num_programs(0)
    R = MOE_BLOCK
    xbufs = (xbuf0, xbuf1)

    def gather_row(block, s, r):
        _row_to_vmem(h_hbm, src_ref[block * R + r], xbufs[s], r, gsem.at[s]).start()

    def wait_gather(s):
        for r in range(R):
            _row_to_vmem(h_hbm, 0, xbufs[s], r, gsem.at[s]).wait()

    @pl.when(i == 0)
    def _():
        def first(r, c):
            gather_row(0, 0, r)
            return c
        lax.fori_loop(0, R, first, 0, unroll=8)

    def step(slot):
        other = 1 - slot
        wait_gather(slot)
        nxt = jnp.minimum(i + 1, nb - 1)

        def issue(r, c):
            gather_row(nxt, other, r)
            return c
        lax.fori_loop(0, R, issue, 0, unroll=8)
        x = xbufs[slot][...].astype(BF16)
        hid = _silu(_dot(x, wg_ref[...])) * _dot(x, wu_ref[...])
        y_ref[...] = _dot(hid.astype(BF16), wd_ref[...])

        @pl.when(i == nb - 1)
        def _():
            wait_gather(other)

    for parity in range(2):
        pl.when(i % 2 == parity)(functools.partial(step, parity))


def _moe_blocks(h2, slot_src, block_exp, wg, wu, wd):
    d = wg.shape[1]
    f = wg.shape[2]
    n_blocks = block_exp.shape[0]
    R = MOE_BLOCK
    grid_spec = pltpu.PrefetchScalarGridSpec(
        num_scalar_prefetch=2,
        grid=(n_blocks,),
        in_specs=[pl.BlockSpec(memory_space=pl.ANY),
                  pl.BlockSpec((None, d, f), lambda i, s, e: (e[i], 0, 0)),
                  pl.BlockSpec((None, d, f), lambda i, s, e: (e[i], 0, 0)),
                  pl.BlockSpec((None, f, d), lambda i, s, e: (e[i], 0, 0))],
        out_specs=pl.BlockSpec((R, d), lambda i, s, e: (i, 0)),
        scratch_shapes=[pltpu.VMEM((R, d), F32), pltpu.VMEM((R, d), F32),
                        pltpu.SemaphoreType.DMA((2,))],
    )
    return pl.pallas_call(
        _moe_kernel,
        grid_spec=grid_spec,
        out_shape=jax.ShapeDtypeStruct((n_blocks * R, d), F32),
        compiler_params=_cp(("arbitrary",)),
    )(slot_src, block_exp, h2, wg, wu, wd)


def _dispatch_plan(idx, t):
    n_assign = t * TOP_K
    e01 = idx[:TOP_K]
    rank01 = idx[TOP_K:2 * TOP_K]
    experts = jnp.arange(N_EXPERTS, dtype=I32)
    onehot = (e01[:, :, None] == experts).astype(I32)
    counts = jnp.sum(onehot, axis=(0, 1))
    padded = (counts + MOE_BLOCK - 1) // MOE_BLOCK * MOE_BLOCK
    pad_end = jnp.cumsum(padded)
    pad_start = pad_end - padded
    dest = jnp.sum(onehot * pad_start, axis=-1) + rank01
    n_blocks = -(-n_assign // MOE_BLOCK) + N_EXPERTS
    n_slots = n_blocks * MOE_BLOCK
    block_start = jnp.arange(n_blocks, dtype=I32) * MOE_BLOCK
    block_exp = jnp.minimum(jnp.sum(block_start[:, None] >= pad_end[None, :], -1), N_EXPERTS - 1).astype(I32)
    tok = jnp.arange(t, dtype=I32)
    slot_src = jnp.zeros((n_slots,), I32).at[dest.reshape(-1)].set(jnp.concatenate([tok, tok]))
    return slot_src, block_exp, dest.reshape(-1)


def kernel(x, c, positions, w_in, conv_w, conv_b, dt_bias, a_log, d_skip, ssd_norm_w, hgrn_gamma,
           hgrn_norm_w, ret_norm_w, ret_norm_b, w_out, ada_down, ada_up, ada_b, ln_g, ln_b,
           router_w, router_b, w_gate, w_up, w_down):
    bsz, seq, d = x.shape
    depth = w_in.shape[0]
    t = bsz * seq
    ssd_w = d // 2
    ssd_heads = ssd_w // SSD_HEAD_DIM
    conv_dim = ssd_w + 2 * SSD_GROUPS * SSD_STATE
    hg_w = d // 4
    ret_w = d // 4
    ret_kw = (ret_w // HEAD_V) * RET_KDIM
    alpha = (2.0 * depth) ** 0.25
    off_dt = ssd_w + conv_dim
    off_hq = off_dt
    off_rq = off_hq + 4 * hg_w

    p = jax.nn.softmax(hgrn_gamma.astype(F32), axis=0)
    lower_bounds = jnp.cumsum(p, axis=0) - p[0]
    mod = _modulation(c, ada_down, ada_up, ada_b)
    cos, sin = _rope_tables(positions)

    rwt = router_w.T
    rw_hi = rwt.astype(BF16)
    rw_lo = (rwt - rw_hi.astype(F32)).astype(BF16)
    rb = router_b.reshape(N_EXPERTS, 1).astype(F32)

    x2 = x.reshape(t, d)
    h = _modulate(x2, mod[0, 1], mod[0, 0], seq)
    for l in range(depth):
        shift1, scale1, gate1, shift2, scale2, gate2 = [mod[l, i] for i in range(N_MOD)]
        w_main = jnp.concatenate([w_in[l][:, :off_dt], w_in[l][:, off_dt + ssd_heads:]], axis=1).astype(BF16)
        w_dt = jnp.pad(w_in[l][:, off_dt:off_dt + ssd_heads], ((0, 0), (0, LANES - ssd_heads))).astype(BF16)
        proj = _matmul(h, w_main, F32)
        dt_raw = _matmul(h, w_dt, F32)
        y_ssd = _ssd(proj, dt_raw, conv_w[l], conv_b[l], dt_bias[l], a_log[l], d_skip[l], ssd_norm_w[l],
                     bsz, seq, ssd_w)
        y_hg = _hgrn(proj, lower_bounds[l], hgrn_norm_w[l], bsz, seq, hg_w, off_hq)
        y_ret = _retention(proj, cos, sin, ret_norm_w[l], ret_norm_b[l], bsz, seq, ret_w, off_rq)
        mixed = _out_proj(y_ssd, y_hg, y_ret, w_out[l].astype(BF16))
        x2, h2, idx, wts = _post_norm(x2, mixed, gate1, ln_g[l, 0], ln_b[l, 0], alpha, seq,
                                      scale=scale2, shift=shift2, h_dtype=F32, router=(rw_hi, rw_lo, rb))
        slot_src, block_exp, pair_slots = _dispatch_plan(idx, t)
        y2 = _moe_blocks(h2, slot_src, block_exp,
                         w_gate[l].astype(BF16), w_up[l].astype(BF16), w_down[l].astype(BF16))
        pair_w = wts[:TOP_K].T
        if l + 1 < depth:
            x2, h = _post_norm(x2, y2, gate2, ln_g[l, 1], ln_b[l, 1], alpha, seq, pair_slots=pair_slots,
                               pair_w=pair_w, scale=mod[l + 1, 1], shift=mod[l + 1, 0])
        else:
            (x2,) = _post_norm(x2, y2, gate2, ln_g[l, 1], ln_b[l, 1], alpha, seq, pair_slots=pair_slots,
                               pair_w=pair_w)
    return x2.reshape(bsz, seq, d)
```
